```python
import math
import jax
import jax.numpy as jnp
from jax import lax
import numpy as np

D_MODEL = 1024
BATCH = 8
SEQ = 2048
DEPTH = 4

GRID_W = 64
CTX_LEN = 256
MIX_WIDTH = D_MODEL
A_WIDTH = MIX_WIDTH // 2
B_WIDTH = MIX_WIDTH - A_WIDTH
SHORT_CONV_W = 3
CONFORMER_CONV_W = 31
AB_IN = 3 * A_WIDTH + 2 * B_WIDTH
C_HEAD_DIM = 64
C_V_DIM = 2 * C_HEAD_DIM
C_HEADS = (MIX_WIDTH // 2) // C_V_DIM
C_QK = C_HEADS * 2 * C_HEAD_DIM
C_V = C_HEADS * C_V_DIM
HY_WIDTH = MIX_WIDTH - C_V
CD_IN = 2 * C_QK + C_V + 3 * HY_WIDTH
Q_BLOCK = 128
ROPE_THETA = 10000.0
HY_BANDS = 16
HY_EMB = 2 * HY_BANDS + 1
HY_ORDER = 64
HY_MIN_DECAY = math.log(1e-2) / 1.5
HY_MAX_DECAY = math.log(1e-2) / 0.3
N_EXPERTS = 16
EXPERT_FF = 1024
EC_CAPACITY = 2
NORM_EPS = 1e-6
SUBLN_EPS = 1e-5

kernel_name = 'hybrid_diffusion_conv_diffattn_hyena_ec_trunk'


def rms_norm(x, g, eps=NORM_EPS):
    xf = x.astype(jnp.float32)
    y = xf * lax.rsqrt(jnp.mean(jnp.square(xf), axis=-1, keepdims=True) + eps)
    return (y * g.astype(jnp.float32)).astype(x.dtype)


def layer_norm(x, g, b, eps=NORM_EPS):
    xf = x.astype(jnp.float32)
    mu = jnp.mean(xf, axis=-1, keepdims=True)
    var = jnp.mean(jnp.square(xf - mu), axis=-1, keepdims=True)
    y = (xf - mu) * lax.rsqrt(var + eps)
    return (y * g.astype(jnp.float32) + b.astype(jnp.float32)).astype(x.dtype)


def modulate(h, shift, scale):
    return h * (1 + scale) + shift


def depthwise_conv(u, w):
    k = w.shape[0]
    pad = (k - 1) // 2
    return lax.conv_general_dilated(
        u, w[:, None, :].astype(u.dtype), window_strides=(1,), padding=[(pad, pad)],
        dimension_numbers=('NWC', 'WIO', 'NWC'), feature_group_count=u.shape[-1])


def conv_mixers(u, w_in, conv_a, conv_b, conv_b_bias, ln_g, ln_b, w_out):
    p = u @ w_in
    gate_out, gate_in, val, glu_a, glu_g = jnp.split(
        p, [A_WIDTH, 2 * A_WIDTH, 3 * A_WIDTH, 3 * A_WIDTH + B_WIDTH], axis=-1)
    y_a = gate_out * depthwise_conv(gate_in * val, conv_a)
    z = depthwise_conv(glu_a * jax.nn.sigmoid(glu_g), conv_b) + conv_b_bias
    y_b = jax.nn.silu(layer_norm(z, ln_g, ln_b))
    return jnp.concatenate([y_a, y_b], axis=-1) @ w_out


def axial_rope_angles(seq_len):
    rows = seq_len // GRID_W
    row = jnp.repeat(jnp.arange(rows), GRID_W).astype(jnp.float32)
    col = jnp.tile(jnp.arange(GRID_W), rows).astype(jnp.float32)
    n_freq = C_HEAD_DIM // 4
    inv = ROPE_THETA ** (-jnp.arange(n_freq, dtype=jnp.float32) / n_freq)
    return row[:, None] * inv[None], col[:, None] * inv[None]


def rope_half(x, ang):
    cos = jnp.cos(ang)[None, :, None, None, :].astype(x.dtype)
    sin = jnp.sin(ang)[None, :, None, None, :].astype(x.dtype)
    x1, x2 = jnp.split(x, 2, axis=-1)
    return jnp.concatenate([x1 * cos - x2 * sin, x1 * sin + x2 * cos], axis=-1)


def axial_rope(x, ang_row, ang_col):
    xr, xc = jnp.split(x, 2, axis=-1)
    return jnp.concatenate([rope_half(xr, ang_row), rope_half(xc, ang_col)], axis=-1)


def split_q(p_q, g_q):
    b, l, _ = p_q.shape
    return rms_norm(p_q.reshape(b, l, C_HEADS, 2, C_HEAD_DIM), g_q)


def split_kv(p_kv, g_k):
    b, l, _ = p_kv.shape
    k, v = jnp.split(p_kv, [C_QK], axis=-1)
    k = rms_norm(k.reshape(b, l, C_HEADS, 2, C_HEAD_DIM), g_k)
    return k, v.reshape(b, l, C_HEADS, C_V_DIM)


def diff_attention(q, k, v, lam):
    s = jnp.einsum('bqhjd,bkhjd->bhjqk', q, k, preferred_element_type=jnp.float32) * (C_HEAD_DIM ** -0.5)
    p = jax.nn.softmax(s, axis=-1)
    a = p[:, :, 0] - lam * p[:, :, 1]
    return jnp.einsum('bhqk,bkhe->bqhe', a.astype(v.dtype), v)


def diff_attention_blocked(q, k, v, lam):
    b, l, h, _, dh = q.shape
    nb = l // Q_BLOCK
    qb = jnp.moveaxis(q.reshape(b, nb, Q_BLOCK, h, 2, dh), 1, 0)
    o = lax.map(lambda qi: diff_attention(qi, k, v, lam), qb)
    return jnp.moveaxis(o, 0, 1).reshape(b, l, h, C_V_DIM)


def diff_head_out(o, g_sub, lam_init):
    b, l = o.shape[:2]
    return (rms_norm(o, g_sub, SUBLN_EPS) * (1.0 - lam_init)).reshape(b, l, C_V)


def hyena_filter(seq_len, w1, b1, freq, w2, b2, w3):
    f32 = jnp.float32
    t = jnp.arange(seq_len, dtype=f32)
    t_unit = jnp.linspace(0.0, 1.0, seq_len)[:, None]
    bands = jnp.linspace(1e-4, HY_BANDS - 1, HY_BANDS)
    ang = (2.0 * math.pi / seq_len) * t[:, None] * bands[None]
    z = jnp.concatenate([t_unit, jnp.cos(ang), -jnp.sin(ang)], axis=-1)
    fr = freq.astype(f32)
    hid = jnp.sin(fr * (z @ w1.astype(f32) + b1.astype(f32)))
    hid = jnp.sin(fr * (hid @ w2.astype(f32) + b2.astype(f32)))
    h = hid @ w3.astype(f32)
    centre = seq_len // 2
    dist = jnp.abs(t - centre) / max(centre, 1)
    decay = jnp.abs(jnp.linspace(HY_MIN_DECAY, HY_MAX_DECAY, h.shape[-1]))
    return h * jnp.exp(-dist[:, None] * decay[None])


def fft_long_conv(u, h, bias):
    l = u.shape[1]
    n = 2 * l
    c0 = l // 2
    uf = jnp.fft.rfft(u.astype(jnp.float32), n=n, axis=1)
    hf = jnp.fft.rfft(h, n=n, axis=0)
    y = jnp.fft.irfft(uf * hf[None], n=n, axis=1)[:, c0:c0 + l]
    return (y + u.astype(jnp.float32) * bias.astype(jnp.float32)).astype(u.dtype)


def hyena_mixer(p_hy, conv_d, filt, bias):
    z = depthwise_conv(p_hy, conv_d)
    gate_out, gate_in, val = jnp.split(z, 3, axis=-1)
    return gate_out * fft_long_conv(gate_in * val, filt, bias)


def expert_choice_ffn(h, w_router, w_gate, w_up, w_down):
    b, n, d = h.shape
    cap = max(1, (EC_CAPACITY * n) // N_EXPERTS)
    aff = jax.nn.softmax((h @ w_router).astype(jnp.float32), axis=-1)
    g, idx = lax.top_k(jnp.swapaxes(aff, 1, 2), cap)
    xe = jax.vmap(lambda hb, ib: hb[ib])(h, idx)
    a = jnp.einsum('becd,edf->becf', xe, w_gate)
    up = jnp.einsum('becd,edf->becf', xe, w_up)
    y = jnp.einsum('becf,efd->becd', jax.nn.silu(a) * up, w_down) * g[..., None].astype(h.dtype)
    return jax.vmap(lambda ib, yb: jnp.zeros((n, d), h.dtype).at[ib.reshape(-1)].add(yb.reshape(-1, d)))(idx, y)


def setup_inputs(seed: int = 0) -> dict:
    key = jax.random.key(seed)
    ks = iter(jax.random.split(key, 48))

    def nrm(shape, scale):
        return jax.random.normal(next(ks), shape, jnp.float32) * scale

    d = D_MODEL
    n_even = (DEPTH + 1) // 2
    n_odd = DEPTH // 2
    return {
        'x': nrm((BATCH, SEQ, d), 1.0),
        'c': nrm((BATCH, d), 1.0),
        'ctx': nrm((BATCH, CTX_LEN, d), 1.0),
        'c_ctx': nrm((d,), 1.0),
        'w_ada': nrm((DEPTH, d, 6 * d), 0.5 * d ** -0.5),
        'b_ada': nrm((DEPTH, 6 * d), 0.02),
        'g_mix': 1.0 + nrm((DEPTH, d), 0.1),
        'g_ffn': 1.0 + nrm((DEPTH, d), 0.1),
        'w_in_ab': nrm((n_even, d, AB_IN), d ** -0.5),
        'conv_a': nrm((n_even, SHORT_CONV_W, A_WIDTH), SHORT_CONV_W ** -0.5),
        'conv_b': nrm((n_even, CONFORMER_CONV_W, B_WIDTH), CONFORMER_CONV_W ** -0.5),
        'conv_b_bias': nrm((n_even, B_WIDTH), 0.02),
        'ln_b_g': 1.0 + nrm((n_even, B_WIDTH), 0.1),
        'ln_b_b': nrm((n_even, B_WIDTH), 0.02),
        'w_out_ab': nrm((n_even, MIX_WIDTH, d), MIX_WIDTH ** -0.5),
        'w_in_cd': nrm((n_odd, d, CD_IN), d ** -0.5),
        'g_q': 1.0 + nrm((n_odd, C_HEAD_DIM), 0.1),
        'g_k': 1.0 + nrm((n_odd, C_HEAD_DIM), 0.1),
        'lam_q1': nrm((n_odd, C_HEAD_DIM), 0.1),
        'lam_k1': nrm((n_odd, C_HEAD_DIM), 0.1),
        'lam_q2': nrm((n_odd, C_HEAD_DIM), 0.1),
        'lam_k2': nrm((n_odd, C_HEAD_DIM), 0.1),
        'g_subln': 1.0 + nrm((n_odd, C_V_DIM), 0.1),
        'conv_d': nrm((n_odd, SHORT_CONV_W, 3 * HY_WIDTH), SHORT_CONV_W ** -0.5),
        'hf_w1': nrm((n_odd, HY_EMB, HY_ORDER), HY_EMB ** -0.5),
        'hf_b1': nrm((n_odd, HY_ORDER), 0.1),
        'hf_freq': 1.0 + nrm((n_odd, HY_ORDER), 0.1),
        'hf_w2': nrm((n_odd, HY_ORDER, HY_ORDER), HY_ORDER ** -0.5),
        'hf_b2': nrm((n_odd, HY_ORDER), 0.1),
        'hf_w3': nrm((n_odd, HY_ORDER, HY_WIDTH), 0.02),
        'hf_bias': nrm((n_odd, HY_WIDTH), 0.1),
        'w_out_cd': nrm((n_odd, MIX_WIDTH, d), MIX_WIDTH ** -0.5),
        'w_router': nrm((DEPTH, d, N_EXPERTS), d ** -0.5),
        'w_gate': nrm((DEPTH, N_EXPERTS, d, EXPERT_FF), d ** -0.5),
        'w_up': nrm((DEPTH, N_EXPERTS, d, EXPERT_FF), d ** -0.5),
        'w_down': nrm((DEPTH, N_EXPERTS, EXPERT_FF, d), EXPERT_FF ** -0.5),
    }


def reference(x, c, ctx, c_ctx, w_ada, b_ada, g_mix, g_ffn, w_in_ab, conv_a, conv_b, conv_b_bias,
              ln_b_g, ln_b_b, w_out_ab, w_in_cd, g_q, g_k, lam_q1, lam_k1, lam_q2, lam_k2, g_subln,
              conv_d, hf_w1, hf_b1, hf_freq, hf_w2, hf_b2, hf_w3, hf_bias, w_out_cd,
              w_router, w_gate, w_up, w_down):
    f32 = jnp.float32
    seq_lat = x.shape[1]
    seq_ctx = ctx.shape[1]
    ang_row, ang_col = axial_rope_angles(seq_lat)
    cond_lat = jax.nn.silu(c)
    cond_ctx = jax.nn.silu(c_ctx)[None]
    h_lat, h_ctx = x, ctx
    kv_cols = slice(C_QK, 2 * C_QK + C_V)
    hy_cols = slice(2 * C_QK + C_V, CD_IN)
    for l in range(DEPTH):
        last = l == DEPTH - 1
        odd = l % 2 == 1
        i = l // 2
        mod_lat = jnp.split((cond_lat @ w_ada[l] + b_ada[l])[:, None, :], 6, axis=-1)
        u_lat = modulate(rms_norm(h_lat, g_mix[l]), mod_lat[0], mod_lat[1])
        if odd or not last:
            mod_ctx = jnp.split((cond_ctx @ w_ada[l] + b_ada[l])[:, None, :], 6, axis=-1)
            u_ctx = modulate(rms_norm(h_ctx, g_mix[l]), mod_ctx[0], mod_ctx[1])
        if not odd:
            ab = (w_in_ab[i], conv_a[i], conv_b[i], conv_b_bias[i], ln_b_g[i], ln_b_b[i], w_out_ab[i])
            y_lat = conv_mixers(u_lat, *ab)
            if not last:
                y_ctx = conv_mixers(u_ctx, *ab)
        else:
            lam_init = 0.8 - 0.6 * math.exp(-0.3 * l)
            lam = (jnp.exp(jnp.sum(lam_q1[i].astype(f32) * lam_k1[i].astype(f32)))
                   - jnp.exp(jnp.sum(lam_q2[i].astype(f32) * lam_k2[i].astype(f32))) + lam_init)
            w_in = w_in_cd[i]
            p_lat = u_lat @ w_in
            q_l = axial_rope(split_q(p_lat[..., :C_QK], g_q[i]), ang_row, ang_col)
            k_l, v_l = split_kv(p_lat[..., kv_cols], g_k[i])
            k_l = axial_rope(k_l, ang_row, ang_col)
            if last:
                k_c, v_c = split_kv(u_ctx @ w_in[:, kv_cols], g_k[i])
            else:
                p_ctx = u_ctx @ w_in
                q_c = split_q(p_ctx[..., :C_QK], g_q[i])
                k_c, v_c = split_kv(p_ctx[..., kv_cols], g_k[i])
            k_all = jnp.concatenate([k_c, k_l], axis=1)
            v_all = jnp.concatenate([v_c, v_l], axis=1)
            o_lat = diff_attention_blocked(q_l, k_all, v_all, lam)
            filt_lat = hyena_filter(seq_lat, hf_w1[i], hf_b1[i], hf_freq[i], hf_w2[i], hf_b2[i], hf_w3[i])
            y_hy_lat = hyena_mixer(p_lat[..., hy_cols], conv_d[i], filt_lat, hf_bias[i])
            y_lat = jnp.concatenate([diff_head_out(o_lat, g_subln[i], lam_init), y_hy_lat], axis=-1) @ w_out_cd[i]
            if not last:
                o_ctx = diff_attention(q_c, k_c, v_c, lam)
                filt_ctx = hyena_filter(seq_ctx, hf_w1[i], hf_b1[i], hf_freq[i], hf_w2[i], hf_b2[i], hf_w3[i])
                y_hy_ctx = hyena_mixer(p_ctx[..., hy_cols], conv_d[i], filt_ctx, hf_bias[i])
                y_ctx = jnp.concatenate([diff_head_out(o_ctx, g_subln[i], lam_init), y_hy_ctx], axis=-1) @ w_out_cd[i]
        h_lat = h_lat + mod_lat[2] * y_lat
        u2 = modulate(rms_norm(h_lat, g_ffn[l]), mod_lat[3], mod_lat[4])
        h_lat = h_lat + mod_lat[5] * expert_choice_ffn(u2, w_router[l], w_gate[l], w_up[l], w_down[l])
        if not last:
            h_ctx = h_ctx + mod_ctx[2] * y_ctx
            u2c = modulate(rms_norm(h_ctx, g_ffn[l]), mod_ctx[3], mod_ctx[4])
            h_ctx = h_ctx + mod_ctx[5] * expert_choice_ffn(u2c, w_router[l], w_gate[l], w_up[l], w_down[l])
    return h_lat
```

```python
import functools
import math

import jax
import jax.numpy as jnp
from jax import lax
from jax.experimental import pallas as pl
from jax.experimental.pallas import tpu as pltpu

F32 = jnp.float32
BF16 = jnp.bfloat16

D_MODEL = 1024
DEPTH = 4
GRID_W = 64
A_WIDTH = 512
B_WIDTH = 512
SHORT_CONV_W = 3
CONFORMER_CONV_W = 31
AB_IN = 3 * A_WIDTH + 2 * B_WIDTH
C_HEAD_DIM = 64
C_V_DIM = 128
C_HEADS = 4
C_QK = 512
C_V = 512
HY_WIDTH = 512
CD_IN = 2 * C_QK + C_V + 3 * HY_WIDTH
ROPE_THETA = 10000.0
HY_BANDS = 16
HY_EMB = 2 * HY_BANDS + 1
HY_MIN_DECAY = math.log(1e-2) / 1.5
HY_MAX_DECAY = math.log(1e-2) / 0.3
N_EXPERTS = 16
EXPERT_FF = 1024
EC_CAPACITY = 2
NORM_EPS = 1e-6
SUBLN_EPS = 1e-5

LANE = 128
SUBLANE = 8
HALO = 16
CONV_ROWS = 128
CONV_PITCH = B_WIDTH + LANE
ROW_CHUNK = 128
MOD_ROWS = 16
CTX_ROW = 8
VMEM_LIMIT = 52 * 1024 * 1024
F32_MIN_NORMAL_BITS = 0x00800000


def _cp(*sem, vmem=VMEM_LIMIT):
    return pltpu.CompilerParams(dimension_semantics=sem, vmem_limit_bytes=vmem)


def _dot(a, b):
    return jnp.dot(a, b, preferred_element_type=F32)


def _dot3(a, b):
    a_hi = a.astype(BF16)
    a_lo = (a - a_hi.astype(F32)).astype(BF16)
    b_hi = b.astype(BF16)
    b_lo = (b - b_hi.astype(F32)).astype(BF16)
    return _dot(a_hi, b_hi) + (_dot(a_hi, b_lo) + _dot(a_lo, b_hi))


def _rms_mod(h, g, shift, scale):
    ms = jnp.mean(h * h, axis=-1, keepdims=True)
    return (h * lax.rsqrt(ms + NORM_EPS) * g) * (1.0 + scale) + shift


def _ada_kernel(c_ref, w_ref, b_ref, o_ref):
    c = c_ref[...]
    s = (c * jax.nn.sigmoid(c)).astype(BF16)
    o_ref[0] = _dot(s, w_ref[0].astype(BF16)) + b_ref[0]


def _ada(cond, w_ada, b_ada):
    depth, d, n = w_ada.shape
    tn = 1536
    out = pl.pallas_call(
        _ada_kernel,
        grid=(depth, n // tn),
        in_specs=[
            pl.BlockSpec((MOD_ROWS, d), lambda l, j: (0, 0)),
            pl.BlockSpec((1, d, tn), lambda l, j: (l, 0, j)),
            pl.BlockSpec((1, 1, tn), lambda l, j: (l, 0, j)),
        ],
        out_specs=pl.BlockSpec((1, MOD_ROWS, tn), lambda l, j: (l, 0, j)),
        out_shape=jax.ShapeDtypeStruct((depth, MOD_ROWS, n), F32),
        compiler_params=_cp("arbitrary", "arbitrary"),
        name="ada",
    )(cond, w_ada, b_ada.reshape(depth, 1, n))
    return out.reshape(depth, MOD_ROWS, 6, d)


def _swap32(x):
    n = x.shape[-1]
    lane = lax.broadcasted_iota(jnp.int32, x.shape, x.ndim - 1)
    up = pltpu.roll(x, n - 32, x.ndim - 1)
    dn = pltpu.roll(x, 32, x.ndim - 1)
    return jnp.where((lane & 32) == 0, up, dn)


def _proj_in_kernel(*refs, n_out, chunk, norm_rows, rope):
    h_ref, mod_ref, g_ref, w_ref = refs[:4]
    k = 4
    if norm_rows:
        gqk_ref, gsum_ref = refs[k:k + 2]
        k += 2
    if rope:
        cos_ref, sin_ref = refs[k:k + 2]
        k += 2
    o_ref, u_scr = refs[k], refs[k + 1]
    mod = mod_ref[0, 0]
    u = _rms_mod(h_ref[0], g_ref[...], mod[0:1], mod[1:2])
    u_scr[...] = u.astype(BF16)
    for n in range(n_out // chunk):
        p = _dot(u_scr[...], w_ref[:, n * chunk:(n + 1) * chunk])
        if n < len(norm_rows):
            ss = _dot((p * p).astype(BF16), gsum_ref[...])
            r = norm_rows[n]
            p = p * lax.rsqrt(ss * (1.0 / C_HEAD_DIM) + NORM_EPS) * gqk_ref[r:r + 1, :]
            if rope:
                p = p * cos_ref[...] + _swap32(p) * sin_ref[...]
        o_ref[0, :, n * chunk:(n + 1) * chunk] = p.astype(o_ref.dtype)


def _proj_in(h, mod, layer, mod_row, g, w, *, norm_rows=(), gqk=None, gsum=None, cos=None, sin=None, name):
    b, l, d = h.shape
    n_out = w.shape[1]
    t = min(512, l)
    chunk = 512
    rope = cos is not None
    mrow = (lambda bb: bb) if mod_row is None else (lambda bb: mod_row)
    in_specs = [
        pl.BlockSpec((1, t, d), lambda i, bb: (bb, i, 0)),
        pl.BlockSpec((1, 1, 6, d), lambda i, bb: (layer, mrow(bb), 0, 0)),
        pl.BlockSpec((1, d), lambda i, bb: (0, 0)),
        pl.BlockSpec((d, n_out), lambda i, bb: (0, 0)),
    ]
    args = [h, mod, g, w]
    if norm_rows:
        in_specs += [pl.BlockSpec(gqk.shape, lambda i, bb: (0, 0)),
                     pl.BlockSpec(gsum.shape, lambda i, bb: (0, 0))]
        args += [gqk, gsum]
    if rope:
        in_specs += [pl.BlockSpec((t, chunk), lambda i, bb: (i, 0))] * 2
        args += [cos, sin]
    return pl.pallas_call(
        functools.partial(_proj_in_kernel, n_out=n_out, chunk=chunk, norm_rows=tuple(norm_rows), rope=rope),
        grid=(l // t, b),
        in_specs=in_specs,
        out_specs=pl.BlockSpec((1, t, n_out), lambda i, bb: (bb, i, 0)),
        out_shape=jax.ShapeDtypeStruct((b, l, n_out), BF16),
        scratch_shapes=[pltpu.VMEM((t, d), BF16)],
        compiler_params=_cp("arbitrary", "arbitrary"),
        name=name,
    )(*args)


def _halo_specs(t, l, width, col):
    r = t // HALO
    last = l // HALO - 1
    return [
        pl.BlockSpec((1, t, width), lambda bb, i: (bb, i, col)),
        pl.BlockSpec((1, HALO, width), lambda bb, i: (bb, jnp.maximum(i * r - 1, 0), col)),
        pl.BlockSpec((1, HALO, width), lambda bb, i: (bb, jnp.minimum((i + 1) * r, last), col)),
    ]


def _conv_mix_kernel(pc_ref, pp_ref, pn_ref, ca_ref, cb_ref, bb_ref, lg_ref, lb_ref, o_ref,
                     abuf, gbuf, zbuf, sbuf, wbuf, *, t, nt):
    i = pl.program_id(1)
    keep_prev = jnp.where(i == 0, 0.0, 1.0)
    keep_next = jnp.where(i == nt - 1, 0.0, 1.0)

    def prods(p):
        p = p.astype(F32)
        a = p[:, A_WIDTH:2 * A_WIDTH] * p[:, 2 * A_WIDTH:3 * A_WIDTH]
        g = p[:, 3 * A_WIDTH:3 * A_WIDTH + B_WIDTH] * jax.nn.sigmoid(p[:, 3 * A_WIDTH + B_WIDTH:])
        return a, g

    a, g = prods(pp_ref[0])
    abuf[0:HALO] = a * keep_prev
    gbuf[0:HALO, 0:B_WIDTH] = g * keep_prev
    a, g = prods(pc_ref[0])
    abuf[HALO:HALO + t] = a
    gbuf[HALO:HALO + t, 0:B_WIDTH] = g
    a, g = prods(pn_ref[0])
    abuf[HALO + t:2 * HALO + t] = a * keep_next
    gbuf[HALO + t:2 * HALO + t, 0:B_WIDTH] = g * keep_next

    ca = ca_ref[...]
    ya = (ca[0:1] * abuf[HALO - 1:HALO - 1 + t] + ca[1:2] * abuf[HALO:HALO + t]
          + ca[2:3] * abuf[HALO + 1:HALO + 1 + t])
    o_ref[0, :, 0:A_WIDTH] = (pc_ref[0, :, 0:A_WIDTH].astype(F32) * ya).astype(o_ref.dtype)

    pad = (CONFORMER_CONV_W - 1) // 2
    rb = CONV_ROWS
    nv = rb // SUBLANE
    gbuf[2 * HALO + t:, 0:B_WIDTH] = jnp.zeros((gbuf.shape[0] - 2 * HALO - t, B_WIDTH), F32)

    def shift_rows(i, carry):
        base = pl.multiple_of(i * rb, rb)
        x = gbuf[pl.ds(base, rb + SUBLANE), 0:B_WIDTH]
        for r in range(1, SUBLANE):
            sbuf[r - 1, pl.ds(base, rb), 0:B_WIDTH] = x[r:r + rb]
        return carry

    lax.fori_loop(0, pl.cdiv(t + 2 * HALO - SUBLANE, rb), shift_rows, 0)
    for k in range(CONFORMER_CONV_W):
        wbuf[k] = jnp.broadcast_to(cb_ref[k:k + 1, :], (SUBLANE, B_WIDTH))

    def conv_rows(i, carry):
        base = pl.multiple_of(i * rb, rb)
        for c0 in range(0, B_WIDTH, LANE):
            acc = jnp.zeros((nv, SUBLANE, LANE), F32) + bb_ref[:, c0:c0 + LANE]
            for k in range(CONFORMER_CONV_W):
                s = HALO - pad + k
                r = s % SUBLANE
                src = gbuf if r == 0 else sbuf.at[r - 1]
                rows = src[pl.ds(base + (s - r), rb), c0:c0 + LANE].reshape(nv, SUBLANE, LANE)
                acc = acc + wbuf[k, :, c0:c0 + LANE] * rows
            zbuf[pl.ds(base, rb), c0:c0 + LANE] = acc.reshape(rb, LANE)
        return carry

    lax.fori_loop(0, t // rb, conv_rows, 0)
    z = zbuf[...]
    mu = jnp.mean(z, axis=-1, keepdims=True)
    zc = z - mu
    var = jnp.mean(zc * zc, axis=-1, keepdims=True)
    y = zc * lax.rsqrt(var + NORM_EPS) * lg_ref[...] + lb_ref[...]
    o_ref[0, :, A_WIDTH:] = (y * jax.nn.sigmoid(y)).astype(o_ref.dtype)


def _conv_mix(p, conv_a, conv_b, conv_b_bias, ln_g, ln_b, *, name):
    b, l, n = p.shape
    t = min(512, l)
    nt = l // t
    full = lambda shape: pl.BlockSpec(shape, lambda bb, i: (0,) * len(shape))
    return pl.pallas_call(
        functools.partial(_conv_mix_kernel, t=t, nt=nt),
        grid=(b, nt),
        in_specs=_halo_specs(t, l, n, 0) + [
            full(conv_a.shape), full(conv_b.shape), full((1, B_WIDTH)), full((1, B_WIDTH)), full((1, B_WIDTH))],
        out_specs=pl.BlockSpec((1, t, A_WIDTH + B_WIDTH), lambda bb, i: (bb, i, 0)),
        out_shape=jax.ShapeDtypeStruct((b, l, A_WIDTH + B_WIDTH), BF16),
        scratch_shapes=[pltpu.VMEM((t + 2 * HALO, A_WIDTH), F32),
                        pltpu.VMEM((t + 2 * HALO + CONV_ROWS, CONV_PITCH), F32),
                        pltpu.VMEM((t, B_WIDTH), F32),
                        pltpu.VMEM((SUBLANE - 1, t + 2 * HALO + CONV_ROWS, CONV_PITCH), F32),
                        pltpu.VMEM((CONFORMER_CONV_W, SUBLANE, B_WIDTH), F32)],
        compiler_params=_cp("arbitrary", "arbitrary"),
        name=name,
    )(p, p, p, conv_a, conv_b, conv_b_bias[None], ln_g[None], ln_b[None])


def _attn_kernel(*refs, lam_init, has_lat):
    if has_lat:
        q_ref, kc_ref, vc_ref, kl_ref, vl_ref, lam_ref, gs_ref, o_ref = refs
    else:
        q_ref, kc_ref, vc_ref, lam_ref, gs_ref, o_ref = refs
    lv = lam_ref[...]
    lam = (jnp.exp(jnp.sum(lv[0:1] * lv[1:2], axis=-1, keepdims=True))
           - jnp.exp(jnp.sum(lv[2:3] * lv[3:4], axis=-1, keepdims=True)) + lam_init)
    q = q_ref[0]
    lane = lax.broadcasted_iota(jnp.int32, q.shape, 1)
    nt = (((1,), (1,)), ((), ()))
    halves = []
    for j in range(2):
        qj = jnp.where((lane >= C_HEAD_DIM) == (j == 1), q, jnp.zeros_like(q))
        s_c = lax.dot_general(qj, kc_ref[0], nt, preferred_element_type=F32)
        m = jnp.max(s_c, axis=-1, keepdims=True)
        if has_lat:
            s_l = lax.dot_general(qj, kl_ref[0], nt, preferred_element_type=F32)
            m = jnp.maximum(m, jnp.max(s_l, axis=-1, keepdims=True))
        e_c = jnp.exp2(s_c - m)
        den = jnp.sum(e_c, axis=-1, keepdims=True)
        oj = _dot(e_c.astype(BF16), vc_ref[0])
        if has_lat:
            e_l = jnp.exp2(s_l - m)
            den = den + jnp.sum(e_l, axis=-1, keepdims=True)
            oj = oj + _dot(e_l.astype(BF16), vl_ref[0])
        halves.append(oj * (1.0 / den))
    o = halves[0] - lam * halves[1]
    ms = jnp.mean(o * o, axis=-1, keepdims=True)
    o = o * lax.rsqrt(ms + SUBLN_EPS) * gs_ref[...] * (1.0 - lam_init)
    o_ref[0] = o.astype(o_ref.dtype)


def _attn(pq, pkv_ctx, kv_off, p_lat, lamv, g_sub, lam_init, *, name):
    b, lq, _ = pq.shape
    lc = pkv_ctx.shape[1]
    tq = min(512, lq)
    has_lat = p_lat is not None
    in_specs = [
        pl.BlockSpec((1, tq, LANE), lambda bb, h, i: (bb, i, h)),
        pl.BlockSpec((1, lc, LANE), lambda bb, h, i: (bb, 0, kv_off + h)),
        pl.BlockSpec((1, lc, LANE), lambda bb, h, i: (bb, 0, kv_off + C_HEADS + h)),
    ]
    args = [pq, pkv_ctx, pkv_ctx]
    if has_lat:
        ll = p_lat.shape[1]
        in_specs += [pl.BlockSpec((1, ll, LANE), lambda bb, h, i: (bb, 0, C_HEADS + h)),
                     pl.BlockSpec((1, ll, LANE), lambda bb, h, i: (bb, 0, 2 * C_HEADS + h))]
        args += [p_lat, p_lat]
    in_specs += [pl.BlockSpec((4, C_HEAD_DIM), lambda bb, h, i: (0, 0)),
                 pl.BlockSpec((1, C_V_DIM), lambda bb, h, i: (0, 0))]
    args += [lamv, g_sub]
    return pl.pallas_call(
        functools.partial(_attn_kernel, lam_init=lam_init, has_lat=has_lat),
        grid=(b, C_HEADS, lq // tq),
        in_specs=in_specs,
        out_specs=pl.BlockSpec((1, tq, LANE), lambda bb, h, i: (bb, i, h)),
        out_shape=jax.ShapeDtypeStruct((b, lq, C_V), BF16),
        compiler_params=_cp("arbitrary", "arbitrary", "arbitrary"),
        name=name,
    )(*args)


def _hy_pre_kernel(pc_ref, pp_ref, pn_ref, cd_ref, o_ref, buf, *, t, nt):
    i = pl.program_id(1)
    buf[0:HALO] = pp_ref[0].astype(F32) * jnp.where(i == 0, 0.0, 1.0)
    buf[HALO:HALO + t] = pc_ref[0].astype(F32)
    buf[HALO + t:2 * HALO + t] = pn_ref[0].astype(F32) * jnp.where(i == nt - 1, 0.0, 1.0)
    cd = cd_ref[...]
    z = (cd[0:1] * buf[HALO - 1:HALO - 1 + t] + cd[1:2] * buf[HALO:HALO + t]
         + cd[2:3] * buf[HALO + 1:HALO + 1 + t])
    o_ref[0, :, 0:HY_WIDTH] = z[:, 0:HY_WIDTH].astype(o_ref.dtype)
    o_ref[0, :, HY_WIDTH:] = (z[:, HY_WIDTH:2 * HY_WIDTH] * z[:, 2 * HY_WIDTH:]).astype(o_ref.dtype)


def _hy_pre(p, conv_d, *, name):
    b, l, _ = p.shape
    t = min(512, l)
    nt = l // t
    w = 3 * HY_WIDTH
    return pl.pallas_call(
        functools.partial(_hy_pre_kernel, t=t, nt=nt),
        grid=(b, nt),
        in_specs=_halo_specs(t, l, w, 1) + [pl.BlockSpec(conv_d.shape, lambda bb, i: (0, 0))],
        out_specs=pl.BlockSpec((1, t, 2 * HY_WIDTH), lambda bb, i: (bb, i, 0)),
        out_shape=jax.ShapeDtypeStruct((b, l, 2 * HY_WIDTH), BF16),
        scratch_shapes=[pltpu.VMEM((t + 2 * HALO, w), F32)],
        compiler_params=_cp("arbitrary", "arbitrary"),
        name=name,
    )(p, p, p, conv_d)


def _hy_filter_kernel(z_ref, w1_ref, b1_ref, fr_ref, w2_ref, b2_ref, w3_ref, dist_ref, dec_ref, o_ref):
    fr = fr_ref[...]
    hid = jnp.sin(fr * (_dot3(z_ref[...], w1_ref[...]) + b1_ref[...]))
    hid = jnp.sin(fr * (_dot3(hid, w2_ref[...]) + b2_ref[...]))
    h = _dot3(hid, w3_ref[...])
    o_ref[...] = h * jnp.exp(-dist_ref[...] * dec_ref[...])


def _hy_filter(seq_len, w1, b1, freq, w2, b2, w3, *, name):
    t = jnp.arange(seq_len, dtype=F32)
    t_unit = jnp.linspace(0.0, 1.0, seq_len)[:, None]
    bands = jnp.linspace(1e-4, HY_BANDS - 1, HY_BANDS)
    ang = (2.0 * math.pi / seq_len) * t[:, None] * bands[None]
    z = jnp.concatenate([t_unit, jnp.cos(ang), -jnp.sin(ang)], axis=-1)
    centre = seq_len // 2
    dist = (jnp.abs(t - centre) / max(centre, 1))[:, None]
    decay = jnp.abs(jnp.linspace(HY_MIN_DECAY, HY_MAX_DECAY, HY_WIDTH))[None]
    pad_c = lambda a, n: jnp.pad(a, ((0, 0), (0, n - a.shape[1])))
    pad_r = lambda a, n: jnp.pad(a, ((0, n - a.shape[0]), (0, 0)))
    args = [pad_c(z, LANE), pad_c(pad_r(w1, LANE), LANE), pad_c(b1[None], LANE), pad_c(freq[None], LANE),
            pad_c(pad_r(w2, LANE), LANE), pad_c(b2[None], LANE), pad_r(w3, LANE), dist, decay]
    return pl.pallas_call(
        _hy_filter_kernel,
        out_shape=jax.ShapeDtypeStruct((seq_len, HY_WIDTH), F32),
        compiler_params=pltpu.CompilerParams(vmem_limit_bytes=VMEM_LIMIT),
        name=name,
    )(*args)


def _dft_mats(l):
    m = 3 * l // 2
    hh = m // 2
    f = jnp.arange(hh, dtype=jnp.int32)
    s = jnp.arange(l, dtype=jnp.int32)
    n = s + l // 2

    def cos_sin(rows, n_cols):
        w = 64
        phase = lambda k: (2.0 * math.pi / m) * ((rows[:, None] * k[None, :]) % m).astype(F32)
        a = phase(jnp.arange(0, n_cols, w, dtype=jnp.int32))[:, :, None]
        b = phase(jnp.arange(w, dtype=jnp.int32))[:, None, :]
        ca, sa, cb, sb = jnp.cos(a), jnp.sin(a), jnp.cos(b), jnp.sin(b)
        shape = (rows.shape[0], n_cols)
        return (ca * cb - sa * sb).reshape(shape), (sa * cb + ca * sb).reshape(shape)

    c, sn = cos_sin(f, l)
    nyq = jnp.where(s % 2 == 0, 1.0, -1.0).astype(F32)
    fw = jnp.stack([c, jnp.where(f[:, None] == 0, nyq[None, :], -sn)]).astype(BF16)
    c, sn = cos_sin(n, hh)
    nyq_n = jnp.where(n % 2 == 0, 1.0, -1.0).astype(F32)
    iv_r = jnp.where(f[None, :] == 0, 1.0, 2.0 * c)
    iv_i = jnp.where(f[None, :] == 0, nyq_n[:, None], -2.0 * sn)
    iv = jnp.concatenate([iv_r, iv_i], axis=1).astype(BF16)
    return fw, iv


def _dft_fwd_kernel(fw_ref, v_ref, h_ref, o_ref, *, tf):
    v = v_ref[0].astype(BF16)
    ur = _dot(fw_ref[0], v)
    ui = _dot(fw_ref[1], v)
    hr = h_ref[0]
    hi = h_ref[1]
    row = lax.broadcasted_iota(jnp.int32, ur.shape, 0) + pl.program_id(0) * tf
    packed = row == 0
    uihi = ui * hi
    o_ref[0, 0] = (ur * hr - jnp.where(packed, 0.0, uihi)).astype(o_ref.dtype)
    o_ref[0, 1] = jnp.where(packed, uihi, ur * hi + ui * hr).astype(o_ref.dtype)


def _dft_fwd(fw, v, col, hspec, out_dtype, *, name):
    b, l, _ = v.shape
    hh = fw.shape[1]
    tf = min(512, hh)
    return pl.pallas_call(
        functools.partial(_dft_fwd_kernel, tf=tf),
        grid=(hh // tf, b),
        in_specs=[
            pl.BlockSpec((2, tf, l), lambda i, bb: (0, i, 0)),
            pl.BlockSpec((1, l, HY_WIDTH), lambda i, bb: (bb, 0, col)),
            pl.BlockSpec((2, tf, HY_WIDTH), lambda i, bb: (0, i, 0)),
        ],
        out_specs=pl.BlockSpec((1, 2, tf, HY_WIDTH), lambda i, bb: (bb, 0, i, 0)),
        out_shape=jax.ShapeDtypeStruct((b, 2, hh, HY_WIDTH), out_dtype),
        compiler_params=_cp("arbitrary", "arbitrary"),
        name=name,
    )(fw, v, hspec)


def _dft_inv_kernel(iv_ref, y_ref, gv_ref, bias_ref, o_ref, *, inv_m):
    m = iv_ref.shape[1]
    yhat = y_ref[0].reshape(m, HY_WIDTH)
    conv = _dot(iv_ref[...], yhat) * inv_m
    gv = gv_ref[0].astype(F32)
    o_ref[0] = (gv[:, 0:HY_WIDTH] * (conv + gv[:, HY_WIDTH:] * bias_ref[...])).astype(o_ref.dtype)


def _dft_inv(iv, yhat, gv, bias, *, name):
    b, l, _ = gv.shape
    m = iv.shape[1]
    tn = min(512, l)
    return pl.pallas_call(
        functools.partial(_dft_inv_kernel, inv_m=1.0 / m),
        grid=(l // tn, b),
        in_specs=[
            pl.BlockSpec((tn, m), lambda i, bb: (i, 0)),
            pl.BlockSpec((1, 2, m // 2, HY_WIDTH), lambda i, bb: (bb, 0, 0, 0)),
            pl.BlockSpec((1, tn, 2 * HY_WIDTH), lambda i, bb: (bb, i, 0)),
            pl.BlockSpec((1, HY_WIDTH), lambda i, bb: (0, 0)),
        ],
        out_specs=pl.BlockSpec((1, tn, HY_WIDTH), lambda i, bb: (bb, i, 0)),
        out_shape=jax.ShapeDtypeStruct((b, l, HY_WIDTH), BF16),
        compiler_params=_cp("arbitrary", "arbitrary"),
        name=name,
    )(iv, yhat, gv, bias)


def _hyena(p, conv_d, filt, bias, dft, tag):
    fw, iv = dft
    hh = fw.shape[1]
    ident = jnp.stack([jnp.ones((hh, HY_WIDTH), F32),
                       jnp.zeros((hh, HY_WIDTH), F32).at[0].set(1.0)])
    fhat = _dft_fwd(fw, filt[None], 0, ident, F32, name="hy_fhat_" + tag)[0]
    gv = _hy_pre(p, conv_d, name="hy_pre_" + tag)
    yhat = _dft_fwd(fw, gv, 1, fhat, BF16, name="hy_fwd_" + tag)
    return _dft_inv(iv, yhat, gv, bias[None], name="hy_inv_" + tag)


def _proj_out_kernel(*refs, n_y):
    y_refs = refs[:n_y]
    w_ref, h_ref, mod_ref, g_ref, wr_ref, h2_ref, u2_ref, aff_ref, afft_ref = refs[n_y:]
    mod = mod_ref[0, 0]
    wr = wr_ref[...]
    wr_hi = wr.astype(BF16)
    wr_lo = (wr - wr_hi.astype(F32)).astype(BF16)
    t = h_ref.shape[1]
    rc = min(ROW_CHUNK, t)
    for r0 in range(0, t, rc):
        acc = None
        k0 = 0
        for y_ref in y_refs:
            kk = y_ref.shape[2]
            part = _dot(y_ref[0, r0:r0 + rc, :], w_ref[k0:k0 + kk, :])
            acc = part if acc is None else acc + part
            k0 += kk
        h2 = h_ref[0, r0:r0 + rc, :] + mod[2:3] * acc
        h2_ref[0, r0:r0 + rc, :] = h2
        u2 = _rms_mod(h2, g_ref[...], mod[3:4], mod[4:5])
        u2_ref[0, r0:r0 + rc, :] = u2.astype(u2_ref.dtype)
        u_hi = u2.astype(BF16)
        u_lo = (u2 - u_hi.astype(F32)).astype(BF16)
        logits = _dot(u_hi, wr_hi) + (_dot(u_hi, wr_lo) + _dot(u_lo, wr_hi))
        lane = lax.broadcasted_iota(jnp.int32, logits.shape, 1)
        logits = jnp.where(lane < N_EXPERTS, logits, -jnp.inf)
        e = jnp.exp(logits - jnp.max(logits, axis=-1, keepdims=True))
        aff = e / jnp.sum(e, axis=-1, keepdims=True)
        aff_ref[0, r0:r0 + rc, :] = aff
        afft_ref[0, :, r0:r0 + rc] = aff.T[0:N_EXPERTS, :]


def _proj_out(ys, w, h, mod, layer, mod_row, g, wr, *, name):
    b, l, d = h.shape
    t = min(512, l)
    mrow = (lambda bb: bb) if mod_row is None else (lambda bb: mod_row)
    in_specs = [pl.BlockSpec((1, t, y.shape[2]), lambda bb, i: (bb, i, 0)) for y in ys] + [
        pl.BlockSpec(w.shape, lambda bb, i: (0, 0)),
        pl.BlockSpec((1, t, d), lambda bb, i: (bb, i, 0)),
        pl.BlockSpec((1, 1, 6, d), lambda bb, i: (layer, mrow(bb), 0, 0)),
        pl.BlockSpec((1, d), lambda bb, i: (0, 0)),
        pl.BlockSpec((d, LANE), lambda bb, i: (0, 0)),
    ]
    return pl.pallas_call(
        functools.partial(_proj_out_kernel, n_y=len(ys)),
        grid=(b, l // t),
        in_specs=in_specs,
        out_specs=[
            pl.BlockSpec((1, t, d), lambda bb, i: (bb, i, 0)),
            pl.BlockSpec((1, t, d), lambda bb, i: (bb, i, 0)),
            pl.BlockSpec((1, t, LANE), lambda bb, i: (bb, i, 0)),
            pl.BlockSpec((1, N_EXPERTS, t), lambda bb, i: (bb, 0, i)),
        ],
        out_shape=[
            jax.ShapeDtypeStruct((b, l, d), F32),
            jax.ShapeDtypeStruct((b, l, d), BF16),
            jax.ShapeDtypeStruct((b, l, LANE), F32),
            jax.ShapeDtypeStruct((b, N_EXPERTS, l), F32),
        ],
        compiler_params=_cp("arbitrary", "arbitrary"),
        name=name,
    )(*ys, w, h, mod, g, wr)


def _excl_cumsum(mask):
    r, l = mask.shape
    x = jnp.where(mask, 1.0, 0.0)
    i0 = lax.broadcasted_iota(jnp.int32, (LANE, LANE), 0)
    i1 = lax.broadcasted_iota(jnp.int32, (LANE, LANE), 1)
    tri = jnp.where(i0 < i1, 1.0, 0.0).astype(BF16)
    carry = jnp.zeros((r, 1), F32)
    outs = []
    for c in range(l // LANE):
        xc = x[:, c * LANE:(c + 1) * LANE]
        outs.append(_dot(xc.astype(BF16), tri) + carry)
        carry = carry + jnp.sum(xc, axis=-1, keepdims=True)
    return jnp.concatenate(outs, axis=1)


def _route_kernel(a_ref, pos_ref, *, cap):
    aff = a_ref[0]
    e, l = aff.shape
    capf = jnp.float32(cap)
    count = lambda mask: jnp.sum(jnp.where(mask, 1.0, 0.0), axis=-1, keepdims=True)

    def body(i, thr):
        cand = thr | jnp.left_shift(jnp.int32(1), 30 - i)
        return jnp.where(count(aff >= pltpu.bitcast(cand, F32)) >= capf, cand, thr)

    thr = lax.fori_loop(0, 31, body, jnp.zeros((e, 1), jnp.int32))
    t_lo = pltpu.bitcast(thr, F32)
    t_hi = pltpu.bitcast(jnp.maximum(thr + 1, F32_MIN_NORMAL_BITS), F32)
    above = aff >= t_hi
    sel0 = jnp.where(above, 1.0, 0.0)
    mid0 = jnp.where((aff >= t_lo) & jnp.logical_not(above), 1.0, 0.0)
    lane = lax.broadcasted_iota(jnp.int32, aff.shape, 1).astype(F32)

    def fill(carry):
        sel, mid, need = carry
        v = jnp.where(mid > 0.0, aff, -1.0)
        best = jnp.max(v, axis=-1, keepdims=True)
        first = jnp.min(jnp.where(v == best, lane, jnp.float32(l)), axis=-1, keepdims=True)
        pick = (lane == first) & (need > 0.0)
        return (jnp.where(pick, 1.0, sel), jnp.where(pick, 0.0, mid), need - jnp.where(need > 0.0, 1.0, 0.0))

    sel, _, _ = lax.while_loop(lambda carry: jnp.max(carry[2]) > 0.0, fill, (sel0, mid0, capf - count(above)))
    chosen = sel > 0.0
    pos = _excl_cumsum(chosen)
    pos_ref[0] = jnp.where(chosen, pos.astype(jnp.int32), -1)


def _route(afft, cap, *, name):
    b, e, l = afft.shape
    return pl.pallas_call(
        functools.partial(_route_kernel, cap=cap),
        grid=(b,),
        in_specs=[pl.BlockSpec((1, e, l), lambda bb: (bb, 0, 0))],
        out_specs=pl.BlockSpec((1, e, l), lambda bb: (bb, 0, 0)),
        out_shape=jax.ShapeDtypeStruct((b, e, l), jnp.int32),
        compiler_params=_cp("arbitrary"),
        name=name,
    )(afft)


def _gather_kernel(u_ref, pos_ref, o_ref, *, eg, cap):
    x = u_ref[0]
    l = x.shape[0]
    slot = lax.broadcasted_iota(jnp.int32, (cap, l), 0)
    for e in range(eg):
        onehot = jnp.where(slot == pos_ref[0, e], 1.0, 0.0).astype(BF16)
        o_ref[0, e] = _dot(onehot, x).astype(o_ref.dtype)


def _gather(u2, pos, cap, *, name):
    b, l, d = u2.shape
    e = pos.shape[1]
    eg = 4
    return pl.pallas_call(
        functools.partial(_gather_kernel, eg=eg, cap=cap),
        grid=(b, e // eg),
        in_specs=[
            pl.BlockSpec((1, l, d), lambda bb, g: (bb, 0, 0)),
            pl.BlockSpec((1, eg, 1, l), lambda bb, g: (bb, g, 0, 0)),
        ],
        out_specs=pl.BlockSpec((1, eg, cap, d), lambda bb, g: (bb, g, 0, 0)),
        out_shape=jax.ShapeDtypeStruct((b, e, cap, d), BF16),
        compiler_params=_cp("arbitrary", "arbitrary"),
        name=name,
    )(u2, pos.reshape(b, e, 1, l))


def _ffn_kernel(*refs, n_x, fc):
    x_refs = refs[:n_x]
    wg_ref, wu_ref, wd_ref = refs[n_x:n_x + 3]
    y_refs = refs[n_x + 3:2 * n_x + 3]
    wg_s, wu_s, wd_s = refs[2 * n_x + 3:]

    @pl.when(pl.program_id(1) == 0)
    def _():
        wg_s[...] = wg_ref[0, 0].astype(BF16)
        wu_s[...] = wu_ref[0, 0].astype(BF16)
        wd_s[...] = wd_ref[0, 0].astype(BF16)

    d = x_refs[0].shape[3]
    rows = [r.shape[0] * r.shape[2] for r in x_refs]
    x = jnp.concatenate([r[:, 0].reshape(n, d) for r, n in zip(x_refs, rows)], axis=0)
    y = None
    for f0 in range(0, wg_s.shape[1], fc):
        a = _dot(x, wg_s[:, f0:f0 + fc])
        up = _dot(x, wu_s[:, f0:f0 + fc])
        hm = (a * jax.nn.sigmoid(a) * up).astype(BF16)
        part = _dot(hm, wd_s[f0:f0 + fc, :])
        y = part if y is None else y + part
    r0 = 0
    for y_ref, n in zip(y_refs, rows):
        y_ref[:, 0] = y[r0:r0 + n].reshape(y_ref.shape[0], y_ref.shape[2], d).astype(y_ref.dtype)
        r0 += n


def _ffn(xs, w_gate, w_up, w_down, layer, *, name):
    b, e, _, d = xs[0].shape
    f = w_gate.shape[3]
    bb = 4
    xspec = lambda x: pl.BlockSpec((bb, 1, x.shape[2], d), lambda ee, j: (j, ee, 0, 0))
    return pl.pallas_call(
        functools.partial(_ffn_kernel, n_x=len(xs), fc=512),
        grid=(e, b // bb),
        in_specs=[xspec(x) for x in xs] + [
            pl.BlockSpec((1, 1, d, f), lambda ee, j: (layer, ee, 0, 0)),
            pl.BlockSpec((1, 1, d, f), lambda ee, j: (layer, ee, 0, 0)),
            pl.BlockSpec((1, 1, f, d), lambda ee, j: (layer, ee, 0, 0)),
        ],
        out_specs=[xspec(x) for x in xs],
        out_shape=[jax.ShapeDtypeStruct(x.shape, BF16) for x in xs],
        scratch_shapes=[pltpu.VMEM((d, f), BF16), pltpu.VMEM((d, f), BF16), pltpu.VMEM((f, d), BF16)],
        compiler_params=_cp("arbitrary", "arbitrary", vmem=56 * 1024 * 1024),
        name=name,
    )(*xs, w_gate, w_up, w_down)


def _combine_kernel(ye_ref, post_ref, aff_ref, h_ref, mod_ref, o_ref, *, cap):
    e = ye_ref.shape[1]
    d = ye_ref.shape[3]
    y = ye_ref[0].reshape(e * cap, d)
    pt = post_ref[0]
    at = aff_ref[0]
    t = pt.shape[0]
    if cap % LANE == 0:
        col = lax.broadcasted_iota(jnp.int32, (t, cap), 1)
        w = jnp.concatenate(
            [jnp.where(pt[:, ee:ee + 1] == col, at[:, ee:ee + 1], 0.0).astype(BF16) for ee in range(e)], axis=1)
    else:
        col = lax.broadcasted_iota(jnp.int32, (t, e * cap), 1)
        acc = jnp.zeros((t, e * cap), F32)
        for ee in range(e):
            slot = pt[:, ee:ee + 1]
            key = jnp.where(slot >= 0, slot + ee * cap, -1)
            acc = jnp.where(key == col, at[:, ee:ee + 1], acc)
        w = acc.astype(BF16)
    o_ref[0] = h_ref[0] + mod_ref[0, 0][5:6] * _dot(w, y)


def _combine(ye, pos_t, aff, h2, mod, layer, mod_row, *, name):
    b, l, d = h2.shape
    e, cap = ye.shape[1], ye.shape[2]
    t = min(512, l)
    mrow = (lambda bb: bb) if mod_row is None else (lambda bb: mod_row)
    return pl.pallas_call(
        functools.partial(_combine_kernel, cap=cap),
        grid=(b, l // t),
        in_specs=[
            pl.BlockSpec((1, e, cap, d), lambda bb, i: (bb, 0, 0, 0)),
            pl.BlockSpec((1, t, e), lambda bb, i: (bb, i, 0)),
            pl.BlockSpec((1, t, LANE), lambda bb, i: (bb, i, 0)),
            pl.BlockSpec((1, t, d), lambda bb, i: (bb, i, 0)),
            pl.BlockSpec((1, 1, 6, d), lambda bb, i: (layer, mrow(bb), 0, 0)),
        ],
        out_specs=pl.BlockSpec((1, t, d), lambda bb, i: (bb, i, 0)),
        out_shape=jax.ShapeDtypeStruct((b, l, d), F32),
        compiler_params=_cp("arbitrary", "arbitrary"),
        name=name,
    )(ye, pos_t, aff, h2, mod)


_PERM64 = tuple(list(range(0, 16)) + list(range(32, 48)) + list(range(16, 32)) + list(range(48, 64)))


def _rope_tables(seq_len):
    t = jnp.arange(seq_len)
    row = (t // GRID_W).astype(F32)
    col = (t % GRID_W).astype(F32)
    n_freq = C_HEAD_DIM // 4
    inv = ROPE_THETA ** (-jnp.arange(n_freq, dtype=F32) / n_freq)
    ar, ac = row[:, None] * inv[None], col[:, None] * inv[None]
    c64 = jnp.concatenate([jnp.cos(ar), jnp.cos(ac), jnp.cos(ar), jnp.cos(ac)], axis=-1)
    s64 = jnp.concatenate([-jnp.sin(ar), -jnp.sin(ac), jnp.sin(ar), jnp.sin(ac)], axis=-1)
    reps = C_QK // C_HEAD_DIM
    return jnp.tile(c64, (1, reps)), jnp.tile(s64, (1, reps))


def kernel(x, c, ctx, c_ctx, w_ada, b_ada, g_mix, g_ffn, w_in_ab, conv_a, conv_b, conv_b_bias, ln_b_g, ln_b_b, w_out_ab, w_in_cd, g_q, g_k, lam_q1, lam_k1, lam_q2, lam_k2, g_subln, conv_d, hf_w1, hf_b1, hf_freq, hf_w2, hf_b2, hf_w3, hf_bias, w_out_cd, w_router, w_gate, w_up, w_down):
    bsz, seq_lat, d = x.shape
    seq_ctx = ctx.shape[1]
    cap_lat = max(1, (EC_CAPACITY * seq_lat) // N_EXPERTS)
    cap_ctx = max(1, (EC_CAPACITY * seq_ctx) // N_EXPERTS)

    cond = jnp.zeros((MOD_ROWS, d), F32).at[:bsz].set(c).at[CTX_ROW].set(c_ctx)
    mod = _ada(cond, w_ada, b_ada)

    perm = jnp.asarray(_PERM64)

    def permute_qk(w):
        quarter = C_HEAD_DIM // 4
        qk = w[:, :2 * C_QK].reshape(d, 2 * C_QK // C_HEAD_DIM, 2, 2, quarter)
        return jnp.concatenate([jnp.swapaxes(qk, 2, 3).reshape(d, 2 * C_QK), w[:, 2 * C_QK:]], axis=1)

    gi = jnp.arange(C_QK) // C_HEAD_DIM
    gsum = (gi[:, None] == gi[None, :]).astype(BF16)
    cos_t, sin_t = _rope_tables(seq_lat)
    dft_lat, dft_ctx = _dft_mats(seq_lat), _dft_mats(seq_ctx)

    h_lat, h_ctx = x, ctx
    for l in range(DEPTH):
        last = l == DEPTH - 1
        odd = l % 2 == 1
        i = l // 2
        run_ctx = odd or not last
        gm = g_mix[l][None]
        if not odd:
            w_in = w_in_ab[i].astype(BF16)
            w_out = w_out_ab[i].astype(BF16)
            cargs = (conv_a[i], conv_b[i], conv_b_bias[i], ln_b_g[i], ln_b_b[i])
            p_lat = _proj_in(h_lat, mod, l, None, gm, w_in, name=f"proj_in_lat{l}")
            ys_lat = [_conv_mix(p_lat, *cargs, name=f"conv_mix_lat{l}")]
            if not last:
                p_ctx = _proj_in(h_ctx, mod, l, CTX_ROW, gm, w_in, name=f"proj_in_ctx{l}")
                ys_ctx = [_conv_mix(p_ctx, *cargs, name=f"conv_mix_ctx{l}")]
        else:
            lam_init = 0.8 - 0.6 * math.exp(-0.3 * l)
            w_in = permute_qk(w_in_cd[i]).astype(BF16)
            w_out = w_out_cd[i].astype(BF16)
            gqk = jnp.stack([jnp.tile(g_q[i][perm], C_QK // C_HEAD_DIM) * (C_HEAD_DIM ** -0.5 * math.log2(math.e)),
                             jnp.tile(g_k[i][perm], C_QK // C_HEAD_DIM)])
            lamv = jnp.stack([lam_q1[i], lam_k1[i], lam_q2[i], lam_k2[i]])
            gsub = g_subln[i][None]
            p_lat = _proj_in(h_lat, mod, l, None, gm, w_in, norm_rows=(0, 1), gqk=gqk, gsum=gsum,
                             cos=cos_t, sin=sin_t, name=f"proj_in_lat{l}")
            if last:
                p_ctx = _proj_in(h_ctx, mod, l, CTX_ROW, gm, w_in[:, C_QK:2 * C_QK + C_V], norm_rows=(1,),
                                 gqk=gqk, gsum=gsum, name=f"proj_in_ctx{l}")
                kv_off = 0
            else:
                p_ctx = _proj_in(h_ctx, mod, l, CTX_ROW, gm, w_in, norm_rows=(0, 1), gqk=gqk, gsum=gsum,
                                 name=f"proj_in_ctx{l}")
                kv_off = C_HEADS
            o_lat = _attn(p_lat, p_ctx, kv_off, p_lat, lamv, gsub, lam_init, name=f"attn_lat{l}")
            fargs = (hf_w1[i], hf_b1[i], hf_freq[i], hf_w2[i], hf_b2[i], hf_w3[i])
            filt_lat = _hy_filter(seq_lat, *fargs, name=f"hy_filter_lat{l}")
            ys_lat = [o_lat, _hyena(p_lat, conv_d[i], filt_lat, hf_bias[i], dft_lat, f"lat{l}")]
            if not last:
                o_ctx = _attn(p_ctx, p_ctx, kv_off, None, lamv, gsub, lam_init, name=f"attn_ctx{l}")
                filt_ctx = _hy_filter(seq_ctx, *fargs, name=f"hy_filter_ctx{l}")
                ys_ctx = [o_ctx, _hyena(p_ctx, conv_d[i], filt_ctx, hf_bias[i], dft_ctx, f"ctx{l}")]

        gf = g_ffn[l][None]
        wr = jnp.pad(w_router[l], ((0, 0), (0, LANE - N_EXPERTS)))
        streams = [("lat", h_lat, ys_lat, None, cap_lat)]
        if not last:
            streams.append(("ctx", h_ctx, ys_ctx, CTX_ROW, cap_ctx))
        staged = []
        for tag, h, ys, mrow, cap in streams:
            h2, u2, aff, afft = _proj_out(ys, w_out, h, mod, l, mrow, gf, wr, name=f"proj_out_{tag}{l}")
            pos = _route(afft, cap, name=f"route_{tag}{l}")
            xe = _gather(u2, pos, cap, name=f"gather_{tag}{l}")
            staged.append((tag, h2, aff, pos, xe, mrow))
        yes = _ffn([s[4] for s in staged], w_gate, w_up, w_down, l, name=f"ffn{l}")
        outs = []
        for (tag, h2, aff, pos, _, mrow), ye in zip(staged, yes):
            pos_t = jnp.transpose(pos, (0, 2, 1))
            outs.append(_combine(ye, pos_t, aff, h2, mod, l, mrow, name=f"combine_{tag}{l}"))
        h_lat = outs[0]
        if not last:
            h_ctx = outs[1]
    return h_lat
```

```python
import functools
import math

import jax
import jax.numpy as jnp
from jax import lax
from jax.experimental import pallas as pl
from jax.experimental.pallas import tpu as pltpu

F32 = jnp.float32
BF16 = jnp.bfloat16

D_MODEL = 1024
DEPTH = 4
GRID_W = 64
A_WIDTH = 512
B_WIDTH = 512
SHORT_CONV_W = 3
CONFORMER_CONV_W = 31
AB_IN = 3 * A_WIDTH + 2 * B_WIDTH
C_HEAD_DIM = 64
C_V_DIM = 128
C_HEADS = 4
C_QK = 512
C_V = 512
HY_WIDTH = 512
CD_IN = 2 * C_QK + C_V + 3 * HY_WIDTH
ROPE_THETA = 10000.0
HY_BANDS = 16
HY_EMB = 2 * HY_BANDS + 1
HY_MIN_DECAY = math.log(1e-2) / 1.5
HY_MAX_DECAY = math.log(1e-2) / 0.3
N_EXPERTS = 16
EXPERT_FF = 1024
EC_CAPACITY = 2
NORM_EPS = 1e-6
SUBLN_EPS = 1e-5

LANE = 128
SUBLANE = 8
HALO = 16
CONV_ROWS = 128
CONV_PITCH = B_WIDTH + LANE
ROW_ALIGN = 16
MOE_CHUNK = 256
MOE_WINDOW = 64
ROW_CHUNK = 128
MOD_ROWS = 16
CTX_ROW = 8
VMEM_LIMIT = 52 * 1024 * 1024
F32_MIN_NORMAL_BITS = 0x00800000


def _cp(*sem, vmem=VMEM_LIMIT):
    return pltpu.CompilerParams(dimension_semantics=sem, vmem_limit_bytes=vmem)


def _dot(a, b):
    return jnp.dot(a, b, preferred_element_type=F32)


def _dot3(a, b):
    a_hi = a.astype(BF16)
    a_lo = (a - a_hi.astype(F32)).astype(BF16)
    b_hi = b.astype(BF16)
    b_lo = (b - b_hi.astype(F32)).astype(BF16)
    return _dot(a_hi, b_hi) + (_dot(a_hi, b_lo) + _dot(a_lo, b_hi))


def _rms_mod(h, g, shift, scale):
    ms = jnp.mean(h * h, axis=-1, keepdims=True)
    return (h * lax.rsqrt(ms + NORM_EPS) * g) * (1.0 + scale) + shift


def _ada_kernel(c_ref, w_ref, b_ref, o_ref):
    c = c_ref[...]
    s = (c * jax.nn.sigmoid(c)).astype(BF16)
    o_ref[0] = _dot(s, w_ref[0].astype(BF16)) + b_ref[0]


def _ada(cond, w_ada, b_ada):
    depth, d, n = w_ada.shape
    tn = 1536
    out = pl.pallas_call(
        _ada_kernel,
        grid=(depth, n // tn),
        in_specs=[
            pl.BlockSpec((MOD_ROWS, d), lambda l, j: (0, 0)),
            pl.BlockSpec((1, d, tn), lambda l, j: (l, 0, j)),
            pl.BlockSpec((1, 1, tn), lambda l, j: (l, 0, j)),
        ],
        out_specs=pl.BlockSpec((1, MOD_ROWS, tn), lambda l, j: (l, 0, j)),
        out_shape=jax.ShapeDtypeStruct((depth, MOD_ROWS, n), F32),
        compiler_params=_cp("arbitrary", "arbitrary"),
        name="ada",
    )(cond, w_ada, b_ada.reshape(depth, 1, n))
    return out.reshape(depth, MOD_ROWS, 6, d)


def _swap32(x):
    n = x.shape[-1]
    lane = lax.broadcasted_iota(jnp.int32, x.shape, x.ndim - 1)
    up = pltpu.roll(x, n - 32, x.ndim - 1)
    dn = pltpu.roll(x, 32, x.ndim - 1)
    return jnp.where((lane & 32) == 0, up, dn)


def _proj_in_kernel(*refs, n_out, chunk, norm_rows, rope):
    h_ref, mod_ref, g_ref, w_ref = refs[:4]
    k = 4
    if norm_rows:
        gqk_ref, gsum_ref = refs[k:k + 2]
        k += 2
    if rope:
        cos_ref, sin_ref = refs[k:k + 2]
        k += 2
    o_ref, u_scr = refs[k], refs[k + 1]
    mod = mod_ref[0, 0]
    u = _rms_mod(h_ref[0], g_ref[...], mod[0:1], mod[1:2])
    u_scr[...] = u.astype(BF16)
    for n in range(n_out // chunk):
        p = _dot(u_scr[...], w_ref[:, n * chunk:(n + 1) * chunk])
        if n < len(norm_rows):
            ss = _dot((p * p).astype(BF16), gsum_ref[...])
            r = norm_rows[n]
            p = p * lax.rsqrt(ss * (1.0 / C_HEAD_DIM) + NORM_EPS) * gqk_ref[r:r + 1, :]
            if rope:
                p = p * cos_ref[...] + _swap32(p) * sin_ref[...]
        o_ref[0, :, n * chunk:(n + 1) * chunk] = p.astype(o_ref.dtype)


def _proj_in(h, mod, layer, mod_row, g, w, *, norm_rows=(), gqk=None, gsum=None, cos=None, sin=None, name):
    b, l, d = h.shape
    n_out = w.shape[1]
    t = min(512, l)
    chunk = 512
    rope = cos is not None
    mrow = (lambda bb: bb) if mod_row is None else (lambda bb: mod_row)
    in_specs = [
        pl.BlockSpec((1, t, d), lambda i, bb: (bb, i, 0)),
        pl.BlockSpec((1, 1, 6, d), lambda i, bb: (layer, mrow(bb), 0, 0)),
        pl.BlockSpec((1, d), lambda i, bb: (0, 0)),
        pl.BlockSpec((d, n_out), lambda i, bb: (0, 0)),
    ]
    args = [h, mod, g, w]
    if norm_rows:
        in_specs += [pl.BlockSpec(gqk.shape, lambda i, bb: (0, 0)),
                     pl.BlockSpec(gsum.shape, lambda i, bb: (0, 0))]
        args += [gqk, gsum]
    if rope:
        in_specs += [pl.BlockSpec((t, chunk), lambda i, bb: (i, 0))] * 2
        args += [cos, sin]
    return pl.pallas_call(
        functools.partial(_proj_in_kernel, n_out=n_out, chunk=chunk, norm_rows=tuple(norm_rows), rope=rope),
        grid=(l // t, b),
        in_specs=in_specs,
        out_specs=pl.BlockSpec((1, t, n_out), lambda i, bb: (bb, i, 0)),
        out_shape=jax.ShapeDtypeStruct((b, l, n_out), BF16),
        scratch_shapes=[pltpu.VMEM((t, d), BF16)],
        compiler_params=_cp("arbitrary", "arbitrary"),
        name=name,
    )(*args)


def _halo_specs(t, l, width, col):
    r = t // HALO
    last = l // HALO - 1
    return [
        pl.BlockSpec((1, t, width), lambda bb, i: (bb, i, col)),
        pl.BlockSpec((1, HALO, width), lambda bb, i: (bb, jnp.maximum(i * r - 1, 0), col)),
        pl.BlockSpec((1, HALO, width), lambda bb, i: (bb, jnp.minimum((i + 1) * r, last), col)),
    ]


def _conv_mix_kernel(pc_ref, pp_ref, pn_ref, ca_ref, cb_ref, bb_ref, lg_ref, lb_ref, o_ref,
                     abuf, gbuf, zbuf, sbuf, wbuf, *, t, nt):
    i = pl.program_id(1)
    keep_prev = jnp.where(i == 0, 0.0, 1.0)
    keep_next = jnp.where(i == nt - 1, 0.0, 1.0)

    def prods(p):
        p = p.astype(F32)
        a = p[:, A_WIDTH:2 * A_WIDTH] * p[:, 2 * A_WIDTH:3 * A_WIDTH]
        g = p[:, 3 * A_WIDTH:3 * A_WIDTH + B_WIDTH] * jax.nn.sigmoid(p[:, 3 * A_WIDTH + B_WIDTH:])
        return a, g

    a, g = prods(pp_ref[0])
    abuf[0:HALO] = a * keep_prev
    gbuf[0:HALO, 0:B_WIDTH] = g * keep_prev
    a, g = prods(pc_ref[0])
    abuf[HALO:HALO + t] = a
    gbuf[HALO:HALO + t, 0:B_WIDTH] = g
    a, g = prods(pn_ref[0])
    abuf[HALO + t:2 * HALO + t] = a * keep_next
    gbuf[HALO + t:2 * HALO + t, 0:B_WIDTH] = g * keep_next

    ca = ca_ref[...]
    ya = (ca[0:1] * abuf[HALO - 1:HALO - 1 + t] + ca[1:2] * abuf[HALO:HALO + t]
          + ca[2:3] * abuf[HALO + 1:HALO + 1 + t])
    o_ref[0, :, 0:A_WIDTH] = (pc_ref[0, :, 0:A_WIDTH].astype(F32) * ya).astype(o_ref.dtype)

    pad = (CONFORMER_CONV_W - 1) // 2
    rb = CONV_ROWS
    nv = rb // SUBLANE
    gbuf[2 * HALO + t:, 0:B_WIDTH] = jnp.zeros((gbuf.shape[0] - 2 * HALO - t, B_WIDTH), F32)

    def shift_rows(i, carry):
        base = pl.multiple_of(i * rb, rb)
        x = gbuf[pl.ds(base, rb + SUBLANE), 0:B_WIDTH]
        for r in range(1, SUBLANE):
            sbuf[r - 1, pl.ds(base, rb), 0:B_WIDTH] = x[r:r + rb]
        return carry

    lax.fori_loop(0, pl.cdiv(t + 2 * HALO - SUBLANE, rb), shift_rows, 0)
    for k in range(CONFORMER_CONV_W):
        wbuf[k] = jnp.broadcast_to(cb_ref[k:k + 1, :], (SUBLANE, B_WIDTH))

    def conv_rows(i, carry):
        base = pl.multiple_of(i * rb, rb)
        for c0 in range(0, B_WIDTH, LANE):
            acc = jnp.zeros((nv, SUBLANE, LANE), F32) + bb_ref[:, c0:c0 + LANE]
            for k in range(CONFORMER_CONV_W):
                s = HALO - pad + k
                r = s % SUBLANE
                src = gbuf if r == 0 else sbuf.at[r - 1]
                rows = src[pl.ds(base + (s - r), rb), c0:c0 + LANE].reshape(nv, SUBLANE, LANE)
                acc = acc + wbuf[k, :, c0:c0 + LANE] * rows
            zbuf[pl.ds(base, rb), c0:c0 + LANE] = acc.reshape(rb, LANE)
        return carry

    lax.fori_loop(0, t // rb, conv_rows, 0)
    z = zbuf[...]
    mu = jnp.mean(z, axis=-1, keepdims=True)
    zc = z - mu
    var = jnp.mean(zc * zc, axis=-1, keepdims=True)
    y = zc * lax.rsqrt(var + NORM_EPS) * lg_ref[...] + lb_ref[...]
    o_ref[0, :, A_WIDTH:] = (y * jax.nn.sigmoid(y)).astype(o_ref.dtype)


def _conv_mix(p, conv_a, conv_b, conv_b_bias, ln_g, ln_b, *, name):
    b, l, n = p.shape
    t = min(512, l)
    nt = l // t
    full = lambda shape: pl.BlockSpec(shape, lambda bb, i: (0,) * len(shape))
    return pl.pallas_call(
        functools.partial(_conv_mix_kernel, t=t, nt=nt),
        grid=(b, nt),
        in_specs=_halo_specs(t, l, n, 0) + [
            full(conv_a.shape), full(conv_b.shape), full((1, B_WIDTH)), full((1, B_WIDTH)), full((1, B_WIDTH))],
        out_specs=pl.BlockSpec((1, t, A_WIDTH + B_WIDTH), lambda bb, i: (bb, i, 0)),
        out_shape=jax.ShapeDtypeStruct((b, l, A_WIDTH + B_WIDTH), BF16),
        scratch_shapes=[pltpu.VMEM((t + 2 * HALO, A_WIDTH), F32),
                        pltpu.VMEM((t + 2 * HALO + CONV_ROWS, CONV_PITCH), F32),
                        pltpu.VMEM((t, B_WIDTH), F32),
                        pltpu.VMEM((SUBLANE - 1, t + 2 * HALO + CONV_ROWS, CONV_PITCH), F32),
                        pltpu.VMEM((CONFORMER_CONV_W, SUBLANE, B_WIDTH), F32)],
        compiler_params=_cp("arbitrary", "arbitrary"),
        name=name,
    )(p, p, p, conv_a, conv_b, conv_b_bias[None], ln_g[None], ln_b[None])


def _attn_kernel(*refs, lam_init, has_lat):
    if has_lat:
        q_ref, kc_ref, vc_ref, kl_ref, vl_ref, lam_ref, gs_ref, o_ref = refs
    else:
        q_ref, kc_ref, vc_ref, lam_ref, gs_ref, o_ref = refs
    lv = lam_ref[...]
    lam = (jnp.exp(jnp.sum(lv[0:1] * lv[1:2], axis=-1, keepdims=True))
           - jnp.exp(jnp.sum(lv[2:3] * lv[3:4], axis=-1, keepdims=True)) + lam_init)
    q = q_ref[0]
    lane = lax.broadcasted_iota(jnp.int32, q.shape, 1)
    nt = (((1,), (1,)), ((), ()))
    halves = []
    for j in range(2):
        qj = jnp.where((lane >= C_HEAD_DIM) == (j == 1), q, jnp.zeros_like(q))
        s_c = lax.dot_general(qj, kc_ref[0], nt, preferred_element_type=F32)
        m = jnp.max(s_c, axis=-1, keepdims=True)
        if has_lat:
            s_l = lax.dot_general(qj, kl_ref[0], nt, preferred_element_type=F32)
            m = jnp.maximum(m, jnp.max(s_l, axis=-1, keepdims=True))
        e_c = jnp.exp2(s_c - m)
        den = jnp.sum(e_c, axis=-1, keepdims=True)
        oj = _dot(e_c.astype(BF16), vc_ref[0])
        if has_lat:
            e_l = jnp.exp2(s_l - m)
            den = den + jnp.sum(e_l, axis=-1, keepdims=True)
            oj = oj + _dot(e_l.astype(BF16), vl_ref[0])
        halves.append(oj * (1.0 / den))
    o = halves[0] - lam * halves[1]
    ms = jnp.mean(o * o, axis=-1, keepdims=True)
    o = o * lax.rsqrt(ms + SUBLN_EPS) * gs_ref[...] * (1.0 - lam_init)
    o_ref[0] = o.astype(o_ref.dtype)


def _attn(pq, pkv_ctx, kv_off, p_lat, lamv, g_sub, lam_init, *, name):
    b, lq, _ = pq.shape
    lc = pkv_ctx.shape[1]
    tq = min(512, lq)
    has_lat = p_lat is not None
    in_specs = [
        pl.BlockSpec((1, tq, LANE), lambda bb, h, i: (bb, i, h)),
        pl.BlockSpec((1, lc, LANE), lambda bb, h, i: (bb, 0, kv_off + h)),
        pl.BlockSpec((1, lc, LANE), lambda bb, h, i: (bb, 0, kv_off + C_HEADS + h)),
    ]
    args = [pq, pkv_ctx, pkv_ctx]
    if has_lat:
        ll = p_lat.shape[1]
        in_specs += [pl.BlockSpec((1, ll, LANE), lambda bb, h, i: (bb, 0, C_HEADS + h)),
                     pl.BlockSpec((1, ll, LANE), lambda bb, h, i: (bb, 0, 2 * C_HEADS + h))]
        args += [p_lat, p_lat]
    in_specs += [pl.BlockSpec((4, C_HEAD_DIM), lambda bb, h, i: (0, 0)),
                 pl.BlockSpec((1, C_V_DIM), lambda bb, h, i: (0, 0))]
    args += [lamv, g_sub]
    return pl.pallas_call(
        functools.partial(_attn_kernel, lam_init=lam_init, has_lat=has_lat),
        grid=(b, C_HEADS, lq // tq),
        in_specs=in_specs,
        out_specs=pl.BlockSpec((1, tq, LANE), lambda bb, h, i: (bb, i, h)),
        out_shape=jax.ShapeDtypeStruct((b, lq, C_V), BF16),
        compiler_params=_cp("arbitrary", "arbitrary", "arbitrary"),
        name=name,
    )(*args)


def _hy_pre_kernel(pc_ref, pp_ref, pn_ref, cd_ref, o_ref, buf, *, t, nt):
    i = pl.program_id(1)
    buf[0:HALO] = pp_ref[0].astype(F32) * jnp.where(i == 0, 0.0, 1.0)
    buf[HALO:HALO + t] = pc_ref[0].astype(F32)
    buf[HALO + t:2 * HALO + t] = pn_ref[0].astype(F32) * jnp.where(i == nt - 1, 0.0, 1.0)
    cd = cd_ref[...]
    z = (cd[0:1] * buf[HALO - 1:HALO - 1 + t] + cd[1:2] * buf[HALO:HALO + t]
         + cd[2:3] * buf[HALO + 1:HALO + 1 + t])
    o_ref[0, :, 0:HY_WIDTH] = z[:, 0:HY_WIDTH].astype(o_ref.dtype)
    o_ref[0, :, HY_WIDTH:] = (z[:, HY_WIDTH:2 * HY_WIDTH] * z[:, 2 * HY_WIDTH:]).astype(o_ref.dtype)


def _hy_pre(p, conv_d, *, name):
    b, l, _ = p.shape
    t = min(512, l)
    nt = l // t
    w = 3 * HY_WIDTH
    return pl.pallas_call(
        functools.partial(_hy_pre_kernel, t=t, nt=nt),
        grid=(b, nt),
        in_specs=_halo_specs(t, l, w, 1) + [pl.BlockSpec(conv_d.shape, lambda bb, i: (0, 0))],
        out_specs=pl.BlockSpec((1, t, 2 * HY_WIDTH), lambda bb, i: (bb, i, 0)),
        out_shape=jax.ShapeDtypeStruct((b, l, 2 * HY_WIDTH), BF16),
        scratch_shapes=[pltpu.VMEM((t + 2 * HALO, w), F32)],
        compiler_params=_cp("arbitrary", "arbitrary"),
        name=name,
    )(p, p, p, conv_d)


def _hy_filter_kernel(z_ref, w1_ref, b1_ref, fr_ref, w2_ref, b2_ref, w3_ref, dist_ref, dec_ref, o_ref):
    fr = fr_ref[...]
    hid = jnp.sin(fr * (_dot3(z_ref[...], w1_ref[...]) + b1_ref[...]))
    hid = jnp.sin(fr * (_dot3(hid, w2_ref[...]) + b2_ref[...]))
    h = _dot3(hid, w3_ref[...])
    o_ref[...] = h * jnp.exp(-dist_ref[...] * dec_ref[...])


def _hy_filter(seq_len, w1, b1, freq, w2, b2, w3, *, name):
    t = jnp.arange(seq_len, dtype=F32)
    t_unit = jnp.linspace(0.0, 1.0, seq_len)[:, None]
    bands = jnp.linspace(1e-4, HY_BANDS - 1, HY_BANDS)
    ang = (2.0 * math.pi / seq_len) * t[:, None] * bands[None]
    z = jnp.concatenate([t_unit, jnp.cos(ang), -jnp.sin(ang)], axis=-1)
    centre = seq_len // 2
    dist = (jnp.abs(t - centre) / max(centre, 1))[:, None]
    decay = jnp.abs(jnp.linspace(HY_MIN_DECAY, HY_MAX_DECAY, HY_WIDTH))[None]
    pad_c = lambda a, n: jnp.pad(a, ((0, 0), (0, n - a.shape[1])))
    pad_r = lambda a, n: jnp.pad(a, ((0, n - a.shape[0]), (0, 0)))
    args = [pad_c(z, LANE), pad_c(pad_r(w1, LANE), LANE), pad_c(b1[None], LANE), pad_c(freq[None], LANE),
            pad_c(pad_r(w2, LANE), LANE), pad_c(b2[None], LANE), pad_r(w3, LANE), dist, decay]
    return pl.pallas_call(
        _hy_filter_kernel,
        out_shape=jax.ShapeDtypeStruct((seq_len, HY_WIDTH), F32),
        compiler_params=pltpu.CompilerParams(vmem_limit_bytes=VMEM_LIMIT),
        name=name,
    )(*args)


def _dft_mats(l):
    m = 3 * l // 2
    hh = m // 2
    f = jnp.arange(hh, dtype=jnp.int32)
    s = jnp.arange(l, dtype=jnp.int32)
    n = s + l // 2

    def cos_sin(rows, n_cols):
        w = 64
        phase = lambda k: (2.0 * math.pi / m) * ((rows[:, None] * k[None, :]) % m).astype(F32)
        a = phase(jnp.arange(0, n_cols, w, dtype=jnp.int32))[:, :, None]
        b = phase(jnp.arange(w, dtype=jnp.int32))[:, None, :]
        ca, sa, cb, sb = jnp.cos(a), jnp.sin(a), jnp.cos(b), jnp.sin(b)
        shape = (rows.shape[0], n_cols)
        return (ca * cb - sa * sb).reshape(shape), (sa * cb + ca * sb).reshape(shape)

    c, sn = cos_sin(f, l)
    nyq = jnp.where(s % 2 == 0, 1.0, -1.0).astype(F32)
    fw = jnp.stack([c, jnp.where(f[:, None] == 0, nyq[None, :], -sn)]).astype(BF16)
    c, sn = cos_sin(n, hh)
    nyq_n = jnp.where(n % 2 == 0, 1.0, -1.0).astype(F32)
    iv_r = jnp.where(f[None, :] == 0, 1.0, 2.0 * c)
    iv_i = jnp.where(f[None, :] == 0, nyq_n[:, None], -2.0 * sn)
    iv = jnp.concatenate([iv_r, iv_i], axis=1).astype(BF16)
    return fw, iv


def _dft_fwd_kernel(fw_ref, v_ref, h_ref, o_ref, *, tf):
    v = v_ref[0].astype(BF16)
    ur = _dot(fw_ref[0], v)
    ui = _dot(fw_ref[1], v)
    hr = h_ref[0]
    hi = h_ref[1]
    row = lax.broadcasted_iota(jnp.int32, ur.shape, 0) + pl.program_id(0) * tf
    packed = row == 0
    uihi = ui * hi
    o_ref[0, 0] = (ur * hr - jnp.where(packed, 0.0, uihi)).astype(o_ref.dtype)
    o_ref[0, 1] = jnp.where(packed, uihi, ur * hi + ui * hr).astype(o_ref.dtype)


def _dft_fwd(fw, v, col, hspec, out_dtype, *, name):
    b, l, _ = v.shape
    hh = fw.shape[1]
    tf = min(512, hh)
    return pl.pallas_call(
        functools.partial(_dft_fwd_kernel, tf=tf),
        grid=(hh // tf, b),
        in_specs=[
            pl.BlockSpec((2, tf, l), lambda i, bb: (0, i, 0)),
            pl.BlockSpec((1, l, HY_WIDTH), lambda i, bb: (bb, 0, col)),
            pl.BlockSpec((2, tf, HY_WIDTH), lambda i, bb: (0, i, 0)),
        ],
        out_specs=pl.BlockSpec((1, 2, tf, HY_WIDTH), lambda i, bb: (bb, 0, i, 0)),
        out_shape=jax.ShapeDtypeStruct((b, 2, hh, HY_WIDTH), out_dtype),
        compiler_params=_cp("arbitrary", "arbitrary"),
        name=name,
    )(fw, v, hspec)


def _dft_inv_kernel(iv_ref, y_ref, gv_ref, bias_ref, o_ref, *, inv_m):
    m = iv_ref.shape[1]
    yhat = y_ref[0].reshape(m, HY_WIDTH)
    conv = _dot(iv_ref[...], yhat) * inv_m
    gv = gv_ref[0].astype(F32)
    o_ref[0] = (gv[:, 0:HY_WIDTH] * (conv + gv[:, HY_WIDTH:] * bias_ref[...])).astype(o_ref.dtype)


def _dft_inv(iv, yhat, gv, bias, *, name):
    b, l, _ = gv.shape
    m = iv.shape[1]
    tn = min(512, l)
    return pl.pallas_call(
        functools.partial(_dft_inv_kernel, inv_m=1.0 / m),
        grid=(l // tn, b),
        in_specs=[
            pl.BlockSpec((tn, m), lambda i, bb: (i, 0)),
            pl.BlockSpec((1, 2, m // 2, HY_WIDTH), lambda i, bb: (bb, 0, 0, 0)),
            pl.BlockSpec((1, tn, 2 * HY_WIDTH), lambda i, bb: (bb, i, 0)),
            pl.BlockSpec((1, HY_WIDTH), lambda i, bb: (0, 0)),
        ],
        out_specs=pl.BlockSpec((1, tn, HY_WIDTH), lambda i, bb: (bb, i, 0)),
        out_shape=jax.ShapeDtypeStruct((b, l, HY_WIDTH), BF16),
        compiler_params=_cp("arbitrary", "arbitrary"),
        name=name,
    )(iv, yhat, gv, bias)


def _hyena(p, conv_d, filt, bias, dft, tag):
    fw, iv = dft
    hh = fw.shape[1]
    ident = jnp.stack([jnp.ones((hh, HY_WIDTH), F32),
                       jnp.zeros((hh, HY_WIDTH), F32).at[0].set(1.0)])
    fhat = _dft_fwd(fw, filt[None], 0, ident, F32, name="hy_fhat_" + tag)[0]
    gv = _hy_pre(p, conv_d, name="hy_pre_" + tag)
    yhat = _dft_fwd(fw, gv, 1, fhat, BF16, name="hy_fwd_" + tag)
    return _dft_inv(iv, yhat, gv, bias[None], name="hy_inv_" + tag)


def _proj_out_kernel(*refs, n_y):
    y_refs = refs[:n_y]
    w_ref, h_ref, mod_ref, g_ref, wr_ref, h2_ref, u2_ref, aff_ref, afft_ref = refs[n_y:]
    mod = mod_ref[0, 0]
    wr = wr_ref[...]
    wr_hi = wr.astype(BF16)
    wr_lo = (wr - wr_hi.astype(F32)).astype(BF16)
    t = h_ref.shape[1]
    rc = min(ROW_CHUNK, t)
    for r0 in range(0, t, rc):
        acc = None
        k0 = 0
        for y_ref in y_refs:
            kk = y_ref.shape[2]
            part = _dot(y_ref[0, r0:r0 + rc, :], w_ref[k0:k0 + kk, :])
            acc = part if acc is None else acc + part
            k0 += kk
        h2 = h_ref[0, r0:r0 + rc, :] + mod[2:3] * acc
        h2_ref[0, r0:r0 + rc, :] = h2
        u2 = _rms_mod(h2, g_ref[...], mod[3:4], mod[4:5])
        u2_ref[0, r0:r0 + rc, :] = u2.astype(u2_ref.dtype)
        u_hi = u2.astype(BF16)
        u_lo = (u2 - u_hi.astype(F32)).astype(BF16)
        logits = _dot(u_hi, wr_hi) + (_dot(u_hi, wr_lo) + _dot(u_lo, wr_hi))
        lane = lax.broadcasted_iota(jnp.int32, logits.shape, 1)
        logits = jnp.where(lane < N_EXPERTS, logits, -jnp.inf)
        e = jnp.exp(logits - jnp.max(logits, axis=-1, keepdims=True))
        aff = e / jnp.sum(e, axis=-1, keepdims=True)
        aff_ref[0, r0:r0 + rc, :] = aff
        afft_ref[0, :, r0:r0 + rc] = aff.T[0:N_EXPERTS, :]


def _proj_out(ys, w, h, mod, layer, mod_row, g, wr, *, name):
    b, l, d = h.shape
    t = min(512, l)
    mrow = (lambda bb: bb) if mod_row is None else (lambda bb: mod_row)
    in_specs = [pl.BlockSpec((1, t, y.shape[2]), lambda bb, i: (bb, i, 0)) for y in ys] + [
        pl.BlockSpec(w.shape, lambda bb, i: (0, 0)),
        pl.BlockSpec((1, t, d), lambda bb, i: (bb, i, 0)),
        pl.BlockSpec((1, 1, 6, d), lambda bb, i: (layer, mrow(bb), 0, 0)),
        pl.BlockSpec((1, d), lambda bb, i: (0, 0)),
        pl.BlockSpec((d, LANE), lambda bb, i: (0, 0)),
    ]
    return pl.pallas_call(
        functools.partial(_proj_out_kernel, n_y=len(ys)),
        grid=(b, l // t),
        in_specs=in_specs,
        out_specs=[
            pl.BlockSpec((1, t, d), lambda bb, i: (bb, i, 0)),
            pl.BlockSpec((1, t, d), lambda bb, i: (bb, i, 0)),
            pl.BlockSpec((1, t, LANE), lambda bb, i: (bb, i, 0)),
            pl.BlockSpec((1, N_EXPERTS, t), lambda bb, i: (bb, 0, i)),
        ],
        out_shape=[
            jax.ShapeDtypeStruct((b, l, d), F32),
            jax.ShapeDtypeStruct((b, l, d), BF16),
            jax.ShapeDtypeStruct((b, l, LANE), F32),
            jax.ShapeDtypeStruct((b, N_EXPERTS, l), F32),
        ],
        compiler_params=_cp("arbitrary", "arbitrary"),
        name=name,
    )(*ys, w, h, mod, g, wr)


def _excl_cumsum(mask):
    r, l = mask.shape
    x = jnp.where(mask, 1.0, 0.0)
    i0 = lax.broadcasted_iota(jnp.int32, (LANE, LANE), 0)
    i1 = lax.broadcasted_iota(jnp.int32, (LANE, LANE), 1)
    tri = jnp.where(i0 < i1, 1.0, 0.0).astype(BF16)
    carry = jnp.zeros((r, 1), F32)
    outs = []
    for c in range(l // LANE):
        xc = x[:, c * LANE:(c + 1) * LANE]
        outs.append(_dot(xc.astype(BF16), tri) + carry)
        carry = carry + jnp.sum(xc, axis=-1, keepdims=True)
    return jnp.concatenate(outs, axis=1)


def _route_kernel(a_ref, pos_ref, starts_ref, *, cap, tc):
    aff = a_ref[0]
    e, l = aff.shape
    capf = jnp.float32(cap)
    count = lambda mask: jnp.sum(jnp.where(mask, 1.0, 0.0), axis=-1, keepdims=True)

    def body(i, thr):
        cand = thr | jnp.left_shift(jnp.int32(1), 30 - i)
        return jnp.where(count(aff >= pltpu.bitcast(cand, F32)) >= capf, cand, thr)

    thr = lax.fori_loop(0, 31, body, jnp.zeros((e, 1), jnp.int32))
    t_lo = pltpu.bitcast(thr, F32)
    t_hi = pltpu.bitcast(jnp.maximum(thr + 1, F32_MIN_NORMAL_BITS), F32)
    above = aff >= t_hi
    sel0 = jnp.where(above, 1.0, 0.0)
    mid0 = jnp.where((aff >= t_lo) & jnp.logical_not(above), 1.0, 0.0)
    lane = lax.broadcasted_iota(jnp.int32, aff.shape, 1).astype(F32)

    def fill(carry):
        sel, mid, need = carry
        v = jnp.where(mid > 0.0, aff, -1.0)
        best = jnp.max(v, axis=-1, keepdims=True)
        first = jnp.min(jnp.where(v == best, lane, jnp.float32(l)), axis=-1, keepdims=True)
        pick = (lane == first) & (need > 0.0)
        return (jnp.where(pick, 1.0, sel), jnp.where(pick, 0.0, mid), need - jnp.where(need > 0.0, 1.0, 0.0))

    sel, _, _ = lax.while_loop(lambda carry: jnp.max(carry[2]) > 0.0, fill, (sel0, mid0, capf - count(above)))
    chosen = sel > 0.0
    pos = _excl_cumsum(chosen)
    pos_ref[0] = jnp.where(chosen, pos.astype(jnp.int32), -1)
    chunk = lax.broadcasted_iota(jnp.int32, (e, LANE), 1)
    starts = jnp.zeros((e, LANE), F32)
    for c in range(l // tc):
        starts = starts + jnp.where(chunk > c, count(chosen[:, c * tc:(c + 1) * tc]), 0.0)
    starts_ref[0] = starts.astype(jnp.int32)


def _route(afft, cap, tc, *, name):
    b, e, l = afft.shape
    pos, starts = pl.pallas_call(
        functools.partial(_route_kernel, cap=cap, tc=tc),
        grid=(b,),
        in_specs=[pl.BlockSpec((1, e, l), lambda bb: (bb, 0, 0))],
        out_specs=[pl.BlockSpec((1, e, l), lambda bb: (bb, 0, 0)),
                   pl.BlockSpec((1, e, LANE), lambda bb: (bb, 0, 0))],
        out_shape=[jax.ShapeDtypeStruct((b, e, l), jnp.int32),
                   jax.ShapeDtypeStruct((b, e, LANE), jnp.int32)],
        compiler_params=_cp("arbitrary"),
        name=name,
    )(afft)
    return pos, starts[:, :, :l // tc + 1].reshape(-1)


def _windows(st_ref, b, c, n_e, n_chunks, cap, win):
    starts, fits = [], None
    for e in range(n_e):
        k = (b * n_e + e) * (n_chunks + 1) + c
        start = jnp.minimum(st_ref[k] & -ROW_ALIGN, cap - win)
        ok = st_ref[k + 1] - start <= win
        starts.append(pl.multiple_of(start, ROW_ALIGN))
        fits = ok if fits is None else fits & ok
    return starts, fits


def _gather_kernel(st_ref, u_ref, pos_ref, o_ref, *, cap, tc, win):
    b = pl.program_id(0)
    n_e = pos_ref.shape[1]
    n_chunks = u_ref.shape[1] // tc
    o_ref[...] = jnp.zeros(o_ref.shape, o_ref.dtype)
    for c in range(n_chunks):
        x = u_ref[0, c * tc:(c + 1) * tc, :]
        pos = pos_ref[0, :, c * tc:(c + 1) * tc]
        starts, fits = _windows(st_ref, b, c, n_e, n_chunks, cap, win)

        @pl.when(fits)
        def _():
            slot = lax.broadcasted_iota(jnp.int32, (win, tc), 0)
            onehot = jnp.concatenate(
                [jnp.where(pos[e:e + 1] == slot + starts[e], 1.0, 0.0).astype(BF16) for e in range(n_e)], axis=0)
            y = _dot(onehot, x).astype(o_ref.dtype)
            for e in range(n_e):
                rows = pl.ds(starts[e], win)
                o_ref[0, e, rows, :] = o_ref[0, e, rows, :] + y[e * win:(e + 1) * win]

        @pl.when(jnp.logical_not(fits))
        def _():
            slot = lax.broadcasted_iota(jnp.int32, (cap, tc), 0)
            for e in range(n_e):
                onehot = jnp.where(pos[e:e + 1] == slot, 1.0, 0.0).astype(BF16)
                o_ref[0, e] = o_ref[0, e] + _dot(onehot, x).astype(o_ref.dtype)


def _gather(u2, pos, starts, cap, tc, win, *, name):
    b, l, d = u2.shape
    e = pos.shape[1]
    return pl.pallas_call(
        functools.partial(_gather_kernel, cap=cap, tc=tc, win=win),
        grid_spec=pltpu.PrefetchScalarGridSpec(
            num_scalar_prefetch=1,
            grid=(b,),
            in_specs=[
                pl.BlockSpec((1, l, d), lambda bb, st: (bb, 0, 0)),
                pl.BlockSpec((1, e, l), lambda bb, st: (bb, 0, 0)),
            ],
            out_specs=pl.BlockSpec((1, e, cap, d), lambda bb, st: (bb, 0, 0, 0)),
        ),
        out_shape=jax.ShapeDtypeStruct((b, e, cap, d), BF16),
        compiler_params=_cp("arbitrary"),
        name=name,
    )(starts, u2, pos)


def _ffn_kernel(*refs, n_x, fc):
    x_refs = refs[:n_x]
    wg_ref, wu_ref, wd_ref = refs[n_x:n_x + 3]
    y_refs = refs[n_x + 3:2 * n_x + 3]
    wg_s, wu_s, wd_s = refs[2 * n_x + 3:]

    @pl.when(pl.program_id(1) == 0)
    def _():
        wg_s[...] = wg_ref[0, 0].astype(BF16)
        wu_s[...] = wu_ref[0, 0].astype(BF16)
        wd_s[...] = wd_ref[0, 0].astype(BF16)

    d = x_refs[0].shape[3]
    rows = [r.shape[0] * r.shape[2] for r in x_refs]
    x = jnp.concatenate([r[:, 0].reshape(n, d) for r, n in zip(x_refs, rows)], axis=0)
    y = None
    for f0 in range(0, wg_s.shape[1], fc):
        a = _dot(x, wg_s[:, f0:f0 + fc])
        up = _dot(x, wu_s[:, f0:f0 + fc])
        hm = (a * jax.nn.sigmoid(a) * up).astype(BF16)
        part = _dot(hm, wd_s[f0:f0 + fc, :])
        y = part if y is None else y + part
    r0 = 0
    for y_ref, n in zip(y_refs, rows):
        y_ref[:, 0] = y[r0:r0 + n].reshape(y_ref.shape[0], y_ref.shape[2], d).astype(y_ref.dtype)
        r0 += n


def _ffn(xs, w_gate, w_up, w_down, layer, *, name):
    b, e, _, d = xs[0].shape
    f = w_gate.shape[3]
    bb = 4
    xspec = lambda x: pl.BlockSpec((bb, 1, x.shape[2], d), lambda ee, j: (j, ee, 0, 0))
    return pl.pallas_call(
        functools.partial(_ffn_kernel, n_x=len(xs), fc=512),
        grid=(e, b // bb),
        in_specs=[xspec(x) for x in xs] + [
            pl.BlockSpec((1, 1, d, f), lambda ee, j: (layer, ee, 0, 0)),
            pl.BlockSpec((1, 1, d, f), lambda ee, j: (layer, ee, 0, 0)),
            pl.BlockSpec((1, 1, f, d), lambda ee, j: (layer, ee, 0, 0)),
        ],
        out_specs=[xspec(x) for x in xs],
        out_shape=[jax.ShapeDtypeStruct(x.shape, BF16) for x in xs],
        scratch_shapes=[pltpu.VMEM((d, f), BF16), pltpu.VMEM((d, f), BF16), pltpu.VMEM((f, d), BF16)],
        compiler_params=_cp("arbitrary", "arbitrary", vmem=56 * 1024 * 1024),
        name=name,
    )(*xs, w_gate, w_up, w_down)


def _combine_kernel(st_ref, ye_ref, post_ref, aff_ref, h_ref, mod_ref, o_ref, *, cap, tc, win, n_chunks):
    b = pl.program_id(0)
    n_e = ye_ref.shape[1]
    d = ye_ref.shape[3]
    t = h_ref.shape[1]
    gate = mod_ref[0, 0][5:6]
    per = LANE // win
    for cc in range(t // tc):
        c = pl.program_id(1) * (t // tc) + cc
        rows = slice(cc * tc, (cc + 1) * tc)
        pt = post_ref[0, rows, :]
        at = aff_ref[0, rows, :]
        starts, fits = _windows(st_ref, b, c, n_e, n_chunks, cap, win)

        def finish(w, y):
            o_ref[0, rows, :] = h_ref[0, rows, :] + gate * _dot(w, y)

        @pl.when(fits)
        def _():
            y = jnp.concatenate([ye_ref[0, e, pl.ds(starts[e], win), :] for e in range(n_e)], axis=0)
            lane = lax.broadcasted_iota(jnp.int32, (tc, LANE), 1)
            blocks = []
            for g in range(n_e // per):
                blk = jnp.zeros((tc, LANE), F32)
                for q in range(per):
                    e = g * per + q
                    slot = pt[:, e:e + 1]
                    key = jnp.where(slot >= 0, slot - starts[e] + q * win, -1)
                    blk = jnp.where(key == lane, at[:, e:e + 1], blk)
                blocks.append(blk.astype(BF16))
            finish(jnp.concatenate(blocks, axis=1), y)

        @pl.when(jnp.logical_not(fits))
        def _():
            y = ye_ref[0].reshape(n_e * cap, d)
            if cap % LANE == 0:
                col = lax.broadcasted_iota(jnp.int32, (tc, cap), 1)
                w = jnp.concatenate(
                    [jnp.where(pt[:, e:e + 1] == col, at[:, e:e + 1], 0.0).astype(BF16) for e in range(n_e)], axis=1)
            else:
                col = lax.broadcasted_iota(jnp.int32, (tc, n_e * cap), 1)
                acc = jnp.zeros((tc, n_e * cap), F32)
                for e in range(n_e):
                    slot = pt[:, e:e + 1]
                    key = jnp.where(slot >= 0, slot + e * cap, -1)
                    acc = jnp.where(key == col, at[:, e:e + 1], acc)
                w = acc.astype(BF16)
            finish(w, y)


def _combine(ye, pos_t, aff, starts, h2, mod, layer, mod_row, tc, win, *, name):
    b, l, d = h2.shape
    e, cap = ye.shape[1], ye.shape[2]
    t = min(512, l)
    mrow = (lambda bb: bb) if mod_row is None else (lambda bb: mod_row)
    return pl.pallas_call(
        functools.partial(_combine_kernel, cap=cap, tc=tc, win=win, n_chunks=l // tc),
        grid_spec=pltpu.PrefetchScalarGridSpec(
            num_scalar_prefetch=1,
            grid=(b, l // t),
            in_specs=[
                pl.BlockSpec((1, e, cap, d), lambda bb, i, st: (bb, 0, 0, 0)),
                pl.BlockSpec((1, t, e), lambda bb, i, st: (bb, i, 0)),
                pl.BlockSpec((1, t, LANE), lambda bb, i, st: (bb, i, 0)),
                pl.BlockSpec((1, t, d), lambda bb, i, st: (bb, i, 0)),
                pl.BlockSpec((1, 1, 6, d), lambda bb, i, st: (layer, mrow(bb), 0, 0)),
            ],
            out_specs=pl.BlockSpec((1, t, d), lambda bb, i, st: (bb, i, 0)),
        ),
        out_shape=jax.ShapeDtypeStruct((b, l, d), F32),
        compiler_params=_cp("arbitrary", "arbitrary"),
        name=name,
    )(starts, ye, pos_t, aff, h2, mod)


_PERM64 = tuple(list(range(0, 16)) + list(range(32, 48)) + list(range(16, 32)) + list(range(48, 64)))


def _rope_tables(seq_len):
    t = jnp.arange(seq_len)
    row = (t // GRID_W).astype(F32)
    col = (t % GRID_W).astype(F32)
    n_freq = C_HEAD_DIM // 4
    inv = ROPE_THETA ** (-jnp.arange(n_freq, dtype=F32) / n_freq)
    ar, ac = row[:, None] * inv[None], col[:, None] * inv[None]
    c64 = jnp.concatenate([jnp.cos(ar), jnp.cos(ac), jnp.cos(ar), jnp.cos(ac)], axis=-1)
    s64 = jnp.concatenate([-jnp.sin(ar), -jnp.sin(ac), jnp.sin(ar), jnp.sin(ac)], axis=-1)
    reps = C_QK // C_HEAD_DIM
    return jnp.tile(c64, (1, reps)), jnp.tile(s64, (1, reps))


def kernel(x, c, ctx, c_ctx, w_ada, b_ada, g_mix, g_ffn, w_in_ab, conv_a, conv_b, conv_b_bias, ln_b_g, ln_b_b, w_out_ab, w_in_cd, g_q, g_k, lam_q1, lam_k1, lam_q2, lam_k2, g_subln, conv_d, hf_w1, hf_b1, hf_freq, hf_w2, hf_b2, hf_w3, hf_bias, w_out_cd, w_router, w_gate, w_up, w_down):
    bsz, seq_lat, d = x.shape
    seq_ctx = ctx.shape[1]
    cap_lat = max(1, (EC_CAPACITY * seq_lat) // N_EXPERTS)
    cap_ctx = max(1, (EC_CAPACITY * seq_ctx) // N_EXPERTS)

    cond = jnp.zeros((MOD_ROWS, d), F32).at[:bsz].set(c).at[CTX_ROW].set(c_ctx)
    mod = _ada(cond, w_ada, b_ada)

    perm = jnp.asarray(_PERM64)

    def permute_qk(w):
        quarter = C_HEAD_DIM // 4
        qk = w[:, :2 * C_QK].reshape(d, 2 * C_QK // C_HEAD_DIM, 2, 2, quarter)
        return jnp.concatenate([jnp.swapaxes(qk, 2, 3).reshape(d, 2 * C_QK), w[:, 2 * C_QK:]], axis=1)

    gi = jnp.arange(C_QK) // C_HEAD_DIM
    gsum = (gi[:, None] == gi[None, :]).astype(BF16)
    cos_t, sin_t = _rope_tables(seq_lat)
    dft_lat, dft_ctx = _dft_mats(seq_lat), _dft_mats(seq_ctx)

    h_lat, h_ctx = x, ctx
    for l in range(DEPTH):
        last = l == DEPTH - 1
        odd = l % 2 == 1
        i = l // 2
        run_ctx = odd or not last
        gm = g_mix[l][None]
        if not odd:
            w_in = w_in_ab[i].astype(BF16)
            w_out = w_out_ab[i].astype(BF16)
            cargs = (conv_a[i], conv_b[i], conv_b_bias[i], ln_b_g[i], ln_b_b[i])
            p_lat = _proj_in(h_lat, mod, l, None, gm, w_in, name=f"proj_in_lat{l}")
            ys_lat = [_conv_mix(p_lat, *cargs, name=f"conv_mix_lat{l}")]
            if not last:
                p_ctx = _proj_in(h_ctx, mod, l, CTX_ROW, gm, w_in, name=f"proj_in_ctx{l}")
                ys_ctx = [_conv_mix(p_ctx, *cargs, name=f"conv_mix_ctx{l}")]
        else:
            lam_init = 0.8 - 0.6 * math.exp(-0.3 * l)
            w_in = permute_qk(w_in_cd[i]).astype(BF16)
            w_out = w_out_cd[i].astype(BF16)
            gqk = jnp.stack([jnp.tile(g_q[i][perm], C_QK // C_HEAD_DIM) * (C_HEAD_DIM ** -0.5 * math.log2(math.e)),
                             jnp.tile(g_k[i][perm], C_QK // C_HEAD_DIM)])
            lamv = jnp.stack([lam_q1[i], lam_k1[i], lam_q2[i], lam_k2[i]])
            gsub = g_subln[i][None]
            p_lat = _proj_in(h_lat, mod, l, None, gm, w_in, norm_rows=(0, 1), gqk=gqk, gsum=gsum,
                             cos=cos_t, sin=sin_t, name=f"proj_in_lat{l}")
            if last:
                p_ctx = _proj_in(h_ctx, mod, l, CTX_ROW, gm, w_in[:, C_QK:2 * C_QK + C_V], norm_rows=(1,),
                                 gqk=gqk, gsum=gsum, name=f"proj_in_ctx{l}")
                kv_off = 0
            else:
                p_ctx = _proj_in(h_ctx, mod, l, CTX_ROW, gm, w_in, norm_rows=(0, 1), gqk=gqk, gsum=gsum,
                                 name=f"proj_in_ctx{l}")
                kv_off = C_HEADS
            o_lat = _attn(p_lat, p_ctx, kv_off, p_lat, lamv, gsub, lam_init, name=f"attn_lat{l}")
            fargs = (hf_w1[i], hf_b1[i], hf_freq[i], hf_w2[i], hf_b2[i], hf_w3[i])
            filt_lat = _hy_filter(seq_lat, *fargs, name=f"hy_filter_lat{l}")
            ys_lat = [o_lat, _hyena(p_lat, conv_d[i], filt_lat, hf_bias[i], dft_lat, f"lat{l}")]
            if not last:
                o_ctx = _attn(p_ctx, p_ctx, kv_off, None, lamv, gsub, lam_init, name=f"attn_ctx{l}")
                filt_ctx = _hy_filter(seq_ctx, *fargs, name=f"hy_filter_ctx{l}")
                ys_ctx = [o_ctx, _hyena(p_ctx, conv_d[i], filt_ctx, hf_bias[i], dft_ctx, f"ctx{l}")]

        gf = g_ffn[l][None]
        wr = jnp.pad(w_router[l], ((0, 0), (0, LANE - N_EXPERTS)))
        streams = [("lat", h_lat, ys_lat, None, cap_lat)]
        if not last:
            streams.append(("ctx", h_ctx, ys_ctx, CTX_ROW, cap_ctx))
        staged = []
        for tag, h, ys, mrow, cap in streams:
            h2, u2, aff, afft = _proj_out(ys, w_out, h, mod, l, mrow, gf, wr, name=f"proj_out_{tag}{l}")
            tc = min(MOE_CHUNK, h.shape[1])
            win = min(MOE_WINDOW, cap)
            pos, starts = _route(afft, cap, tc, name=f"route_{tag}{l}")
            xe = _gather(u2, pos, starts, cap, tc, win, name=f"gather_{tag}{l}")
            staged.append((tag, h2, aff, pos, starts, xe, mrow, tc, win))
        yes = _ffn([s[5] for s in staged], w_gate, w_up, w_down, l, name=f"ffn{l}")
        outs = []
        for (tag, h2, aff, pos, starts, _, mrow, tc, win), ye in zip(staged, yes):
            pos_t = jnp.transpose(pos, (0, 2, 1))
            outs.append(_combine(ye, pos_t, aff, starts, h2, mod, l, mrow, tc, win, name=f"combine_{tag}{l}"))
        h_lat = outs[0]
        if not last:
            h_ctx = outs[1]
    return h_lat
```

```python
import functools
import math

import jax
import jax.numpy as jnp
from jax import lax
from jax.experimental import pallas as pl
from jax.experimental.pallas import tpu as pltpu

F32 = jnp.float32
BF16 = jnp.bfloat16

D_MODEL = 1024
DEPTH = 4
GRID_W = 64
A_WIDTH = 512
B_WIDTH = 512
SHORT_CONV_W = 3
CONFORMER_CONV_W = 31
AB_IN = 3 * A_WIDTH + 2 * B_WIDTH
C_HEAD_DIM = 64
C_V_DIM = 128
C_HEADS = 4
C_QK = 512
C_V = 512
HY_WIDTH = 512
CD_IN = 2 * C_QK + C_V + 3 * HY_WIDTH
ROPE_THETA = 10000.0
HY_BANDS = 16
HY_EMB = 2 * HY_BANDS + 1
HY_MIN_DECAY = math.log(1e-2) / 1.5
HY_MAX_DECAY = math.log(1e-2) / 0.3
N_EXPERTS = 16
EXPERT_FF = 1024
EC_CAPACITY = 2
NORM_EPS = 1e-6
SUBLN_EPS = 1e-5

LANE = 128
SUBLANE = 8
HALO = 16
CONV_ROWS = 128
CONV_PITCH = B_WIDTH + LANE
ROW_ALIGN = 16
MOE_CHUNK = 256
MOE_WINDOW = 64
ATTN_ROWS = 128
ROW_CHUNK = 128
MOD_ROWS = 16
CTX_ROW = 8
VMEM_LIMIT = 52 * 1024 * 1024
F32_MIN_NORMAL_BITS = 0x00800000


def _cp(*sem, vmem=VMEM_LIMIT):
    return pltpu.CompilerParams(dimension_semantics=sem, vmem_limit_bytes=vmem)


def _dot(a, b):
    return jnp.dot(a, b, preferred_element_type=F32)


def _dot3(a, b):
    a_hi = a.astype(BF16)
    a_lo = (a - a_hi.astype(F32)).astype(BF16)
    b_hi = b.astype(BF16)
    b_lo = (b - b_hi.astype(F32)).astype(BF16)
    return _dot(a_hi, b_hi) + (_dot(a_hi, b_lo) + _dot(a_lo, b_hi))


def _rms_mod(h, g, shift, scale):
    ms = jnp.mean(h * h, axis=-1, keepdims=True)
    return (h * lax.rsqrt(ms + NORM_EPS) * g) * (1.0 + scale) + shift


def _ada_kernel(c_ref, w_ref, b_ref, o_ref):
    c = c_ref[...]
    s = (c * jax.nn.sigmoid(c)).astype(BF16)
    o_ref[0] = _dot(s, w_ref[0].astype(BF16)) + b_ref[0]


def _ada(cond, w_ada, b_ada):
    depth, d, n = w_ada.shape
    tn = 1536
    out = pl.pallas_call(
        _ada_kernel,
        grid=(depth, n // tn),
        in_specs=[
            pl.BlockSpec((MOD_ROWS, d), lambda l, j: (0, 0)),
            pl.BlockSpec((1, d, tn), lambda l, j: (l, 0, j)),
            pl.BlockSpec((1, 1, tn), lambda l, j: (l, 0, j)),
        ],
        out_specs=pl.BlockSpec((1, MOD_ROWS, tn), lambda l, j: (l, 0, j)),
        out_shape=jax.ShapeDtypeStruct((depth, MOD_ROWS, n), F32),
        compiler_params=_cp("arbitrary", "arbitrary"),
        name="ada",
    )(cond, w_ada, b_ada.reshape(depth, 1, n))
    return out.reshape(depth, MOD_ROWS, 6, d)


def _swap32(x):
    n = x.shape[-1]
    lane = lax.broadcasted_iota(jnp.int32, x.shape, x.ndim - 1)
    up = pltpu.roll(x, n - 32, x.ndim - 1)
    dn = pltpu.roll(x, 32, x.ndim - 1)
    return jnp.where((lane & 32) == 0, up, dn)


def _proj_in_kernel(*refs, n_out, chunk, norm_rows, rope):
    h_ref, mod_ref, g_ref, w_ref = refs[:4]
    k = 4
    if norm_rows:
        gqk_ref, gsum_ref = refs[k:k + 2]
        k += 2
    if rope:
        cos_ref, sin_ref = refs[k:k + 2]
        k += 2
    o_ref, u_scr = refs[k], refs[k + 1]
    mod = mod_ref[0, 0]
    u = _rms_mod(h_ref[0], g_ref[...], mod[0:1], mod[1:2])
    u_scr[...] = u.astype(BF16)
    for n in range(n_out // chunk):
        p = _dot(u_scr[...], w_ref[:, n * chunk:(n + 1) * chunk])
        if n < len(norm_rows):
            ss = _dot((p * p).astype(BF16), gsum_ref[...])
            r = norm_rows[n]
            p = p * lax.rsqrt(ss * (1.0 / C_HEAD_DIM) + NORM_EPS) * gqk_ref[r:r + 1, :]
            if rope:
                p = p * cos_ref[...] + _swap32(p) * sin_ref[...]
        o_ref[0, :, n * chunk:(n + 1) * chunk] = p.astype(o_ref.dtype)


def _proj_in(h, mod, layer, mod_row, g, w, *, norm_rows=(), gqk=None, gsum=None, cos=None, sin=None, name):
    b, l, d = h.shape
    n_out = w.shape[1]
    t = min(512, l)
    chunk = 512
    rope = cos is not None
    mrow = (lambda bb: bb) if mod_row is None else (lambda bb: mod_row)
    in_specs = [
        pl.BlockSpec((1, t, d), lambda i, bb: (bb, i, 0)),
        pl.BlockSpec((1, 1, 6, d), lambda i, bb: (layer, mrow(bb), 0, 0)),
        pl.BlockSpec((1, d), lambda i, bb: (0, 0)),
        pl.BlockSpec((d, n_out), lambda i, bb: (0, 0)),
    ]
    args = [h, mod, g, w]
    if norm_rows:
        in_specs += [pl.BlockSpec(gqk.shape, lambda i, bb: (0, 0)),
                     pl.BlockSpec(gsum.shape, lambda i, bb: (0, 0))]
        args += [gqk, gsum]
    if rope:
        in_specs += [pl.BlockSpec((t, chunk), lambda i, bb: (i, 0))] * 2
        args += [cos, sin]
    return pl.pallas_call(
        functools.partial(_proj_in_kernel, n_out=n_out, chunk=chunk, norm_rows=tuple(norm_rows), rope=rope),
        grid=(l // t, b),
        in_specs=in_specs,
        out_specs=pl.BlockSpec((1, t, n_out), lambda i, bb: (bb, i, 0)),
        out_shape=jax.ShapeDtypeStruct((b, l, n_out), BF16),
        scratch_shapes=[pltpu.VMEM((t, d), BF16)],
        compiler_params=_cp("arbitrary", "arbitrary"),
        name=name,
    )(*args)


def _halo_specs(t, l, width, col):
    r = t // HALO
    last = l // HALO - 1
    return [
        pl.BlockSpec((1, t, width), lambda bb, i: (bb, i, col)),
        pl.BlockSpec((1, HALO, width), lambda bb, i: (bb, jnp.maximum(i * r - 1, 0), col)),
        pl.BlockSpec((1, HALO, width), lambda bb, i: (bb, jnp.minimum((i + 1) * r, last), col)),
    ]


def _conv_mix_kernel(pc_ref, pp_ref, pn_ref, ca_ref, cb_ref, bb_ref, lg_ref, lb_ref, o_ref,
                     abuf, gbuf, zbuf, sbuf, wbuf, *, t, nt):
    i = pl.program_id(1)
    keep_prev = jnp.where(i == 0, 0.0, 1.0)
    keep_next = jnp.where(i == nt - 1, 0.0, 1.0)

    def prods(p):
        p = p.astype(F32)
        a = p[:, A_WIDTH:2 * A_WIDTH] * p[:, 2 * A_WIDTH:3 * A_WIDTH]
        g = p[:, 3 * A_WIDTH:3 * A_WIDTH + B_WIDTH] * jax.nn.sigmoid(p[:, 3 * A_WIDTH + B_WIDTH:])
        return a, g

    a, g = prods(pp_ref[0])
    abuf[0:HALO] = a * keep_prev
    gbuf[0:HALO, 0:B_WIDTH] = g * keep_prev
    a, g = prods(pc_ref[0])
    abuf[HALO:HALO + t] = a
    gbuf[HALO:HALO + t, 0:B_WIDTH] = g
    a, g = prods(pn_ref[0])
    abuf[HALO + t:2 * HALO + t] = a * keep_next
    gbuf[HALO + t:2 * HALO + t, 0:B_WIDTH] = g * keep_next

    ca = ca_ref[...]
    ya = (ca[0:1] * abuf[HALO - 1:HALO - 1 + t] + ca[1:2] * abuf[HALO:HALO + t]
          + ca[2:3] * abuf[HALO + 1:HALO + 1 + t])
    o_ref[0, :, 0:A_WIDTH] = (pc_ref[0, :, 0:A_WIDTH].astype(F32) * ya).astype(o_ref.dtype)

    pad = (CONFORMER_CONV_W - 1) // 2
    rb = CONV_ROWS
    nv = rb // SUBLANE
    gbuf[2 * HALO + t:, 0:B_WIDTH] = jnp.zeros((gbuf.shape[0] - 2 * HALO - t, B_WIDTH), F32)

    def shift_rows(i, carry):
        base = pl.multiple_of(i * rb, rb)
        x = gbuf[pl.ds(base, rb + SUBLANE), 0:B_WIDTH]
        for r in range(1, SUBLANE):
            sbuf[r - 1, pl.ds(base, rb), 0:B_WIDTH] = x[r:r + rb]
        return carry

    lax.fori_loop(0, pl.cdiv(t + 2 * HALO - SUBLANE, rb), shift_rows, 0)
    for k in range(CONFORMER_CONV_W):
        wbuf[k] = jnp.broadcast_to(cb_ref[k:k + 1, :], (SUBLANE, B_WIDTH))

    def conv_rows(i, carry):
        base = pl.multiple_of(i * rb, rb)
        for c0 in range(0, B_WIDTH, LANE):
            acc = jnp.zeros((nv, SUBLANE, LANE), F32) + bb_ref[:, c0:c0 + LANE]
            for k in range(CONFORMER_CONV_W):
                s = HALO - pad + k
                r = s % SUBLANE
                src = gbuf if r == 0 else sbuf.at[r - 1]
                rows = src[pl.ds(base + (s - r), rb), c0:c0 + LANE].reshape(nv, SUBLANE, LANE)
                acc = acc + wbuf[k, :, c0:c0 + LANE] * rows
            zbuf[pl.ds(base, rb), c0:c0 + LANE] = acc.reshape(rb, LANE)
        return carry

    lax.fori_loop(0, t // rb, conv_rows, 0)
    z = zbuf[...]
    mu = jnp.mean(z, axis=-1, keepdims=True)
    zc = z - mu
    var = jnp.mean(zc * zc, axis=-1, keepdims=True)
    y = zc * lax.rsqrt(var + NORM_EPS) * lg_ref[...] + lb_ref[...]
    o_ref[0, :, A_WIDTH:] = (y * jax.nn.sigmoid(y)).astype(o_ref.dtype)


def _conv_mix(p, conv_a, conv_b, conv_b_bias, ln_g, ln_b, *, name):
    b, l, n = p.shape
    t = min(512, l)
    nt = l // t
    full = lambda shape: pl.BlockSpec(shape, lambda bb, i: (0,) * len(shape))
    return pl.pallas_call(
        functools.partial(_conv_mix_kernel, t=t, nt=nt),
        grid=(b, nt),
        in_specs=_halo_specs(t, l, n, 0) + [
            full(conv_a.shape), full(conv_b.shape), full((1, B_WIDTH)), full((1, B_WIDTH)), full((1, B_WIDTH))],
        out_specs=pl.BlockSpec((1, t, A_WIDTH + B_WIDTH), lambda bb, i: (bb, i, 0)),
        out_shape=jax.ShapeDtypeStruct((b, l, A_WIDTH + B_WIDTH), BF16),
        scratch_shapes=[pltpu.VMEM((t + 2 * HALO, A_WIDTH), F32),
                        pltpu.VMEM((t + 2 * HALO + CONV_ROWS, CONV_PITCH), F32),
                        pltpu.VMEM((t, B_WIDTH), F32),
                        pltpu.VMEM((SUBLANE - 1, t + 2 * HALO + CONV_ROWS, CONV_PITCH), F32),
                        pltpu.VMEM((CONFORMER_CONV_W, SUBLANE, B_WIDTH), F32)],
        compiler_params=_cp("arbitrary", "arbitrary"),
        name=name,
    )(p, p, p, conv_a, conv_b, conv_b_bias[None], ln_g[None], ln_b[None])


def _attn_kernel(*refs, lam_init, has_lat):
    if has_lat:
        q_ref, kc_ref, vc_ref, kl_ref, vl_ref, lam_ref, gs_ref, o_ref = refs
    else:
        q_ref, kc_ref, vc_ref, lam_ref, gs_ref, o_ref = refs
    lv = lam_ref[...]
    lam = (jnp.exp(jnp.sum(lv[0:1] * lv[1:2], axis=-1, keepdims=True))
           - jnp.exp(jnp.sum(lv[2:3] * lv[3:4], axis=-1, keepdims=True)) + lam_init)
    q = q_ref[0]
    lane = lax.broadcasted_iota(jnp.int32, q.shape, 1)
    nt = (((1,), (1,)), ((), ()))
    ones_col = lambda v: jnp.concatenate([v, jnp.ones_like(v)], axis=1)
    vc = ones_col(vc_ref[0])
    if has_lat:
        vl = ones_col(vl_ref[0])
    halves = []
    tq = q.shape[0]
    rc = min(ATTN_ROWS, tq)
    for j in range(2):
        qj = jnp.where((lane >= C_HEAD_DIM) == (j == 1), q, jnp.zeros_like(q))
        outs = []
        for r0 in range(0, tq, rc):
            qr = qj[r0:r0 + rc]
            s_c = lax.dot_general(qr, kc_ref[0], nt, preferred_element_type=F32)
            m = jnp.max(s_c, axis=-1, keepdims=True)
            if has_lat:
                s_l = lax.dot_general(qr, kl_ref[0], nt, preferred_element_type=F32)
                m = jnp.maximum(m, jnp.max(s_l, axis=-1, keepdims=True))
            oj = _dot(jnp.exp2(s_c - m).astype(BF16), vc)
            if has_lat:
                oj = oj + _dot(jnp.exp2(s_l - m).astype(BF16), vl)
            outs.append(oj[:, 0:C_V_DIM] * (1.0 / oj[:, C_V_DIM:C_V_DIM + 1]))
        halves.append(jnp.concatenate(outs, axis=0))
    o = halves[0] - lam * halves[1]
    ms = jnp.mean(o * o, axis=-1, keepdims=True)
    o = o * lax.rsqrt(ms + SUBLN_EPS) * gs_ref[...] * (1.0 - lam_init)
    o_ref[0] = o.astype(o_ref.dtype)


def _attn(pq, pkv_ctx, kv_off, p_lat, lamv, g_sub, lam_init, *, name):
    b, lq, _ = pq.shape
    lc = pkv_ctx.shape[1]
    tq = min(512, lq)
    has_lat = p_lat is not None
    in_specs = [
        pl.BlockSpec((1, tq, LANE), lambda bb, h, i: (bb, i, h)),
        pl.BlockSpec((1, lc, LANE), lambda bb, h, i: (bb, 0, kv_off + h)),
        pl.BlockSpec((1, lc, LANE), lambda bb, h, i: (bb, 0, kv_off + C_HEADS + h)),
    ]
    args = [pq, pkv_ctx, pkv_ctx]
    if has_lat:
        ll = p_lat.shape[1]
        in_specs += [pl.BlockSpec((1, ll, LANE), lambda bb, h, i: (bb, 0, C_HEADS + h)),
                     pl.BlockSpec((1, ll, LANE), lambda bb, h, i: (bb, 0, 2 * C_HEADS + h))]
        args += [p_lat, p_lat]
    in_specs += [pl.BlockSpec((4, C_HEAD_DIM), lambda bb, h, i: (0, 0)),
                 pl.BlockSpec((1, C_V_DIM), lambda bb, h, i: (0, 0))]
    args += [lamv, g_sub]
    return pl.pallas_call(
        functools.partial(_attn_kernel, lam_init=lam_init, has_lat=has_lat),
        grid=(b, C_HEADS, lq // tq),
        in_specs=in_specs,
        out_specs=pl.BlockSpec((1, tq, LANE), lambda bb, h, i: (bb, i, h)),
        out_shape=jax.ShapeDtypeStruct((b, lq, C_V), BF16),
        compiler_params=_cp("arbitrary", "arbitrary", "arbitrary"),
        name=name,
    )(*args)


def _hy_pre_kernel(pc_ref, pp_ref, pn_ref, cd_ref, o_ref, buf, *, t, nt):
    i = pl.program_id(1)
    buf[0:HALO] = pp_ref[0].astype(F32) * jnp.where(i == 0, 0.0, 1.0)
    buf[HALO:HALO + t] = pc_ref[0].astype(F32)
    buf[HALO + t:2 * HALO + t] = pn_ref[0].astype(F32) * jnp.where(i == nt - 1, 0.0, 1.0)
    cd = cd_ref[...]
    z = (cd[0:1] * buf[HALO - 1:HALO - 1 + t] + cd[1:2] * buf[HALO:HALO + t]
         + cd[2:3] * buf[HALO + 1:HALO + 1 + t])
    o_ref[0, :, 0:HY_WIDTH] = z[:, 0:HY_WIDTH].astype(o_ref.dtype)
    o_ref[0, :, HY_WIDTH:] = (z[:, HY_WIDTH:2 * HY_WIDTH] * z[:, 2 * HY_WIDTH:]).astype(o_ref.dtype)


def _hy_pre(p, conv_d, *, name):
    b, l, _ = p.shape
    t = min(512, l)
    nt = l // t
    w = 3 * HY_WIDTH
    return pl.pallas_call(
        functools.partial(_hy_pre_kernel, t=t, nt=nt),
        grid=(b, nt),
        in_specs=_halo_specs(t, l, w, 1) + [pl.BlockSpec(conv_d.shape, lambda bb, i: (0, 0))],
        out_specs=pl.BlockSpec((1, t, 2 * HY_WIDTH), lambda bb, i: (bb, i, 0)),
        out_shape=jax.ShapeDtypeStruct((b, l, 2 * HY_WIDTH), BF16),
        scratch_shapes=[pltpu.VMEM((t + 2 * HALO, w), F32)],
        compiler_params=_cp("arbitrary", "arbitrary"),
        name=name,
    )(p, p, p, conv_d)


def _hy_filter_kernel(z_ref, w1_ref, b1_ref, fr_ref, w2_ref, b2_ref, w3_ref, dist_ref, dec_ref, o_ref):
    fr = fr_ref[...]
    hid = jnp.sin(fr * (_dot3(z_ref[...], w1_ref[...]) + b1_ref[...]))
    hid = jnp.sin(fr * (_dot3(hid, w2_ref[...]) + b2_ref[...]))
    h = _dot3(hid, w3_ref[...])
    o_ref[...] = h * jnp.exp(-dist_ref[...] * dec_ref[...])


def _hy_filter(seq_len, w1, b1, freq, w2, b2, w3, *, name):
    t = jnp.arange(seq_len, dtype=F32)
    t_unit = jnp.linspace(0.0, 1.0, seq_len)[:, None]
    bands = jnp.linspace(1e-4, HY_BANDS - 1, HY_BANDS)
    ang = (2.0 * math.pi / seq_len) * t[:, None] * bands[None]
    z = jnp.concatenate([t_unit, jnp.cos(ang), -jnp.sin(ang)], axis=-1)
    centre = seq_len // 2
    dist = (jnp.abs(t - centre) / max(centre, 1))[:, None]
    decay = jnp.abs(jnp.linspace(HY_MIN_DECAY, HY_MAX_DECAY, HY_WIDTH))[None]
    pad_c = lambda a, n: jnp.pad(a, ((0, 0), (0, n - a.shape[1])))
    pad_r = lambda a, n: jnp.pad(a, ((0, n - a.shape[0]), (0, 0)))
    args = [pad_c(z, LANE), pad_c(pad_r(w1, LANE), LANE), pad_c(b1[None], LANE), pad_c(freq[None], LANE),
            pad_c(pad_r(w2, LANE), LANE), pad_c(b2[None], LANE), pad_r(w3, LANE), dist, decay]
    return pl.pallas_call(
        _hy_filter_kernel,
        out_shape=jax.ShapeDtypeStruct((seq_len, HY_WIDTH), F32),
        compiler_params=pltpu.CompilerParams(vmem_limit_bytes=VMEM_LIMIT),
        name=name,
    )(*args)


def _dft_mats(l):
    m = 3 * l // 2
    hh = m // 2
    f = jnp.arange(hh, dtype=jnp.int32)
    s = jnp.arange(l, dtype=jnp.int32)
    n = s + l // 2

    def cos_sin(rows, n_cols):
        w = 64
        phase = lambda k: (2.0 * math.pi / m) * ((rows[:, None] * k[None, :]) % m).astype(F32)
        a = phase(jnp.arange(0, n_cols, w, dtype=jnp.int32))[:, :, None]
        b = phase(jnp.arange(w, dtype=jnp.int32))[:, None, :]
        ca, sa, cb, sb = jnp.cos(a), jnp.sin(a), jnp.cos(b), jnp.sin(b)
        shape = (rows.shape[0], n_cols)
        return (ca * cb - sa * sb).reshape(shape), (sa * cb + ca * sb).reshape(shape)

    c, sn = cos_sin(f, l)
    nyq = jnp.where(s % 2 == 0, 1.0, -1.0).astype(F32)
    fw = jnp.stack([c, jnp.where(f[:, None] == 0, nyq[None, :], -sn)]).astype(BF16)
    c, sn = cos_sin(n, hh)
    nyq_n = jnp.where(n % 2 == 0, 1.0, -1.0).astype(F32)
    iv_r = jnp.where(f[None, :] == 0, 1.0, 2.0 * c)
    iv_i = jnp.where(f[None, :] == 0, nyq_n[:, None], -2.0 * sn)
    iv = jnp.concatenate([iv_r, iv_i], axis=1).astype(BF16)
    return fw, iv


def _dft_fwd_kernel(fw_ref, v_ref, h_ref, o_ref, *, tf):
    v = v_ref[0].astype(BF16)
    ur = _dot(fw_ref[0], v)
    ui = _dot(fw_ref[1], v)
    hr = h_ref[0]
    hi = h_ref[1]
    row = lax.broadcasted_iota(jnp.int32, ur.shape, 0) + pl.program_id(0) * tf
    packed = row == 0
    uihi = ui * hi
    o_ref[0, 0] = (ur * hr - jnp.where(packed, 0.0, uihi)).astype(o_ref.dtype)
    o_ref[0, 1] = jnp.where(packed, uihi, ur * hi + ui * hr).astype(o_ref.dtype)


def _dft_fwd(fw, v, col, hspec, out_dtype, *, name):
    b, l, _ = v.shape
    hh = fw.shape[1]
    tf = min(512, hh)
    return pl.pallas_call(
        functools.partial(_dft_fwd_kernel, tf=tf),
        grid=(hh // tf, b),
        in_specs=[
            pl.BlockSpec((2, tf, l), lambda i, bb: (0, i, 0)),
            pl.BlockSpec((1, l, HY_WIDTH), lambda i, bb: (bb, 0, col)),
            pl.BlockSpec((2, tf, HY_WIDTH), lambda i, bb: (0, i, 0)),
        ],
        out_specs=pl.BlockSpec((1, 2, tf, HY_WIDTH), lambda i, bb: (bb, 0, i, 0)),
        out_shape=jax.ShapeDtypeStruct((b, 2, hh, HY_WIDTH), out_dtype),
        compiler_params=_cp("arbitrary", "arbitrary"),
        name=name,
    )(fw, v, hspec)


def _dft_inv_kernel(iv_ref, y_ref, gv_ref, bias_ref, o_ref, *, inv_m):
    m = iv_ref.shape[1]
    yhat = y_ref[0].reshape(m, HY_WIDTH)
    conv = _dot(iv_ref[...], yhat) * inv_m
    gv = gv_ref[0].astype(F32)
    o_ref[0] = (gv[:, 0:HY_WIDTH] * (conv + gv[:, HY_WIDTH:] * bias_ref[...])).astype(o_ref.dtype)


def _dft_inv(iv, yhat, gv, bias, *, name):
    b, l, _ = gv.shape
    m = iv.shape[1]
    tn = min(512, l)
    return pl.pallas_call(
        functools.partial(_dft_inv_kernel, inv_m=1.0 / m),
        grid=(l // tn, b),
        in_specs=[
            pl.BlockSpec((tn, m), lambda i, bb: (i, 0)),
            pl.BlockSpec((1, 2, m // 2, HY_WIDTH), lambda i, bb: (bb, 0, 0, 0)),
            pl.BlockSpec((1, tn, 2 * HY_WIDTH), lambda i, bb: (bb, i, 0)),
            pl.BlockSpec((1, HY_WIDTH), lambda i, bb: (0, 0)),
        ],
        out_specs=pl.BlockSpec((1, tn, HY_WIDTH), lambda i, bb: (bb, i, 0)),
        out_shape=jax.ShapeDtypeStruct((b, l, HY_WIDTH), BF16),
        compiler_params=_cp("arbitrary", "arbitrary"),
        name=name,
    )(iv, yhat, gv, bias)


def _hyena(p, conv_d, filt, bias, dft, tag):
    fw, iv = dft
    hh = fw.shape[1]
    ident = jnp.stack([jnp.ones((hh, HY_WIDTH), F32),
                       jnp.zeros((hh, HY_WIDTH), F32).at[0].set(1.0)])
    fhat = _dft_fwd(fw, filt[None], 0, ident, F32, name="hy_fhat_" + tag)[0]
    gv = _hy_pre(p, conv_d, name="hy_pre_" + tag)
    yhat = _dft_fwd(fw, gv, 1, fhat, BF16, name="hy_fwd_" + tag)
    return _dft_inv(iv, yhat, gv, bias[None], name="hy_inv_" + tag)


def _proj_out_kernel(*refs, n_y):
    y_refs = refs[:n_y]
    w_ref, h_ref, mod_ref, g_ref, wr_ref, h2_ref, u2_ref, aff_ref, afft_ref = refs[n_y:]
    mod = mod_ref[0, 0]
    wr = wr_ref[...]
    wr_hi = wr.astype(BF16).astype(F32)
    wr_hl = (wr_hi + pltpu.roll(wr - wr_hi, N_EXPERTS, 1)).astype(BF16)
    t = h_ref.shape[1]
    rc = min(ROW_CHUNK, t)
    for r0 in range(0, t, rc):
        acc = None
        k0 = 0
        for y_ref in y_refs:
            kk = y_ref.shape[2]
            part = _dot(y_ref[0, r0:r0 + rc, :], w_ref[k0:k0 + kk, :])
            acc = part if acc is None else acc + part
            k0 += kk
        h2 = h_ref[0, r0:r0 + rc, :] + mod[2:3] * acc
        h2_ref[0, r0:r0 + rc, :] = h2
        u2 = _rms_mod(h2, g_ref[...], mod[3:4], mod[4:5])
        u2_ref[0, r0:r0 + rc, :] = u2.astype(u2_ref.dtype)
        u_hi = u2.astype(BF16)
        u_lo = (u2 - u_hi.astype(F32)).astype(BF16)
        parts = _dot(u_hi, wr_hl) + _dot(u_lo, wr_hl)
        logits = parts + pltpu.roll(parts, LANE - N_EXPERTS, 1)
        lane = lax.broadcasted_iota(jnp.int32, logits.shape, 1)
        logits = jnp.where(lane < N_EXPERTS, logits, -jnp.inf)
        e = jnp.exp(logits - jnp.max(logits, axis=-1, keepdims=True))
        aff = e / jnp.sum(e, axis=-1, keepdims=True)
        aff_ref[0, r0:r0 + rc, :] = aff
        afft_ref[0, :, r0:r0 + rc] = aff.T[0:N_EXPERTS, :]


def _proj_out(ys, w, h, mod, layer, mod_row, g, wr, *, name):
    b, l, d = h.shape
    t = min(512, l)
    mrow = (lambda bb: bb) if mod_row is None else (lambda bb: mod_row)
    in_specs = [pl.BlockSpec((1, t, y.shape[2]), lambda bb, i: (bb, i, 0)) for y in ys] + [
        pl.BlockSpec(w.shape, lambda bb, i: (0, 0)),
        pl.BlockSpec((1, t, d), lambda bb, i: (bb, i, 0)),
        pl.BlockSpec((1, 1, 6, d), lambda bb, i: (layer, mrow(bb), 0, 0)),
        pl.BlockSpec((1, d), lambda bb, i: (0, 0)),
        pl.BlockSpec((d, LANE), lambda bb, i: (0, 0)),
    ]
    return pl.pallas_call(
        functools.partial(_proj_out_kernel, n_y=len(ys)),
        grid=(b, l // t),
        in_specs=in_specs,
        out_specs=[
            pl.BlockSpec((1, t, d), lambda bb, i: (bb, i, 0)),
            pl.BlockSpec((1, t, d), lambda bb, i: (bb, i, 0)),
            pl.BlockSpec((1, t, LANE), lambda bb, i: (bb, i, 0)),
            pl.BlockSpec((1, N_EXPERTS, t), lambda bb, i: (bb, 0, i)),
        ],
        out_shape=[
            jax.ShapeDtypeStruct((b, l, d), F32),
            jax.ShapeDtypeStruct((b, l, d), BF16),
            jax.ShapeDtypeStruct((b, l, LANE), F32),
            jax.ShapeDtypeStruct((b, N_EXPERTS, l), F32),
        ],
        compiler_params=_cp("arbitrary", "arbitrary"),
        name=name,
    )(*ys, w, h, mod, g, wr)


def _excl_cumsum(mask):
    r, l = mask.shape
    x = jnp.where(mask, 1.0, 0.0)
    i0 = lax.broadcasted_iota(jnp.int32, (LANE, LANE), 0)
    i1 = lax.broadcasted_iota(jnp.int32, (LANE, LANE), 1)
    tri = jnp.where(i0 < i1, 1.0, 0.0).astype(BF16)
    carry = jnp.zeros((r, 1), F32)
    outs = []
    for c in range(l // LANE):
        xc = x[:, c * LANE:(c + 1) * LANE]
        outs.append(_dot(xc.astype(BF16), tri) + carry)
        carry = carry + jnp.sum(xc, axis=-1, keepdims=True)
    return jnp.concatenate(outs, axis=1)


def _route_kernel(a_ref, pos_ref, starts_ref, *, cap, tc):
    aff = a_ref[...]
    e, l = aff.shape
    capf = jnp.float32(cap)
    count = lambda mask: jnp.sum(jnp.where(mask, 1.0, 0.0), axis=-1, keepdims=True)

    def body(i, thr):
        cand = thr | jnp.left_shift(jnp.int32(1), 30 - i)
        return jnp.where(count(aff >= pltpu.bitcast(cand, F32)) >= capf, cand, thr)

    thr = lax.fori_loop(0, 31, body, jnp.zeros((e, 1), jnp.int32))
    t_lo = pltpu.bitcast(thr, F32)
    t_hi = pltpu.bitcast(jnp.maximum(thr + 1, F32_MIN_NORMAL_BITS), F32)
    above = aff >= t_hi
    sel0 = jnp.where(above, 1.0, 0.0)
    mid0 = jnp.where((aff >= t_lo) & jnp.logical_not(above), 1.0, 0.0)
    lane = lax.broadcasted_iota(jnp.int32, aff.shape, 1).astype(F32)

    def fill(carry):
        sel, mid, need = carry
        v = jnp.where(mid > 0.0, aff, -1.0)
        best = jnp.max(v, axis=-1, keepdims=True)
        first = jnp.min(jnp.where(v == best, lane, jnp.float32(l)), axis=-1, keepdims=True)
        pick = (lane == first) & (need > 0.0)
        return (jnp.where(pick, 1.0, sel), jnp.where(pick, 0.0, mid), need - jnp.where(need > 0.0, 1.0, 0.0))

    sel, _, _ = lax.while_loop(lambda carry: jnp.max(carry[2]) > 0.0, fill, (sel0, mid0, capf - count(above)))
    chosen = sel > 0.0
    pos = _excl_cumsum(chosen)
    pos_ref[...] = jnp.where(chosen, pos.astype(jnp.int32), -1)
    chunk = lax.broadcasted_iota(jnp.int32, (e, LANE), 1)
    starts = jnp.zeros((e, LANE), F32)
    for c in range(l // tc):
        starts = starts + jnp.where(chunk > c, count(chosen[:, c * tc:(c + 1) * tc]), 0.0)
    starts_ref[...] = starts.astype(jnp.int32)


def _route(afft, cap, tc, *, name):
    b, e, l = afft.shape
    pos, starts = pl.pallas_call(
        functools.partial(_route_kernel, cap=cap, tc=tc),
        out_shape=[jax.ShapeDtypeStruct((b * e, l), jnp.int32),
                   jax.ShapeDtypeStruct((b * e, LANE), jnp.int32)],
        compiler_params=pltpu.CompilerParams(vmem_limit_bytes=VMEM_LIMIT),
        name=name,
    )(afft.reshape(b * e, l))
    return pos.reshape(b, e, l), starts[:, :l // tc + 1].reshape(-1)


def _windows(st_ref, b, c, n_e, n_chunks, cap, win):
    starts, fits = [], None
    for e in range(n_e):
        k = (b * n_e + e) * (n_chunks + 1) + c
        start = jnp.minimum(st_ref[k] & -ROW_ALIGN, cap - win)
        ok = st_ref[k + 1] - start <= win
        starts.append(pl.multiple_of(start, ROW_ALIGN))
        fits = ok if fits is None else fits & ok
    return starts, fits


def _gather_kernel(st_ref, u_ref, pos_ref, o_ref, *, cap, tc, win):
    b = pl.program_id(0)
    n_e = pos_ref.shape[1]
    n_chunks = u_ref.shape[1] // tc
    o_ref[...] = jnp.zeros(o_ref.shape, o_ref.dtype)
    for c in range(n_chunks):
        x = u_ref[0, c * tc:(c + 1) * tc, :]
        pos = pos_ref[0, :, c * tc:(c + 1) * tc]
        starts, fits = _windows(st_ref, b, c, n_e, n_chunks, cap, win)

        @pl.when(fits)
        def _():
            slot = lax.broadcasted_iota(jnp.int32, (win, tc), 0)
            onehot = jnp.concatenate(
                [jnp.where(pos[e:e + 1] == slot + starts[e], 1.0, 0.0).astype(BF16) for e in range(n_e)], axis=0)
            y = _dot(onehot, x).astype(o_ref.dtype)
            for e in range(n_e):
                rows = pl.ds(starts[e], win)
                o_ref[0, e, rows, :] = o_ref[0, e, rows, :] + y[e * win:(e + 1) * win]

        @pl.when(jnp.logical_not(fits))
        def _():
            slot = lax.broadcasted_iota(jnp.int32, (cap, tc), 0)
            for e in range(n_e):
                onehot = jnp.where(pos[e:e + 1] == slot, 1.0, 0.0).astype(BF16)
                o_ref[0, e] = o_ref[0, e] + _dot(onehot, x).astype(o_ref.dtype)


def _gather(u2, pos, starts, cap, tc, win, *, name):
    b, l, d = u2.shape
    e = pos.shape[1]
    return pl.pallas_call(
        functools.partial(_gather_kernel, cap=cap, tc=tc, win=win),
        grid_spec=pltpu.PrefetchScalarGridSpec(
            num_scalar_prefetch=1,
            grid=(b,),
            in_specs=[
                pl.BlockSpec((1, l, d), lambda bb, st: (bb, 0, 0)),
                pl.BlockSpec((1, e, l), lambda bb, st: (bb, 0, 0)),
            ],
            out_specs=pl.BlockSpec((1, e, cap, d), lambda bb, st: (bb, 0, 0, 0)),
        ),
        out_shape=jax.ShapeDtypeStruct((b, e, cap, d), BF16),
        compiler_params=_cp("arbitrary"),
        name=name,
    )(starts, u2, pos)


def _ffn_kernel(*refs, n_x, fc):
    x_refs = refs[:n_x]
    wg_ref, wu_ref, wd_ref = refs[n_x:n_x + 3]
    y_refs = refs[n_x + 3:2 * n_x + 3]
    wg_s, wu_s, wd_s = refs[2 * n_x + 3:]

    @pl.when(pl.program_id(1) == 0)
    def _():
        wg_s[...] = wg_ref[0, 0].astype(BF16)
        wu_s[...] = wu_ref[0, 0].astype(BF16)
        wd_s[...] = wd_ref[0, 0].astype(BF16)

    d = x_refs[0].shape[3]
    rows = [r.shape[0] * r.shape[2] for r in x_refs]
    x = jnp.concatenate([r[:, 0].reshape(n, d) for r, n in zip(x_refs, rows)], axis=0)
    y = None
    for f0 in range(0, wg_s.shape[1], fc):
        a = _dot(x, wg_s[:, f0:f0 + fc])
        up = _dot(x, wu_s[:, f0:f0 + fc])
        hm = (a * jax.nn.sigmoid(a) * up).astype(BF16)
        part = _dot(hm, wd_s[f0:f0 + fc, :])
        y = part if y is None else y + part
    r0 = 0
    for y_ref, n in zip(y_refs, rows):
        y_ref[:, 0] = y[r0:r0 + n].reshape(y_ref.shape[0], y_ref.shape[2], d).astype(y_ref.dtype)
        r0 += n


def _ffn(xs, w_gate, w_up, w_down, layer, *, name):
    b, e, _, d = xs[0].shape
    f = w_gate.shape[3]
    bb = 4
    xspec = lambda x: pl.BlockSpec((bb, 1, x.shape[2], d), lambda ee, j: (j, ee, 0, 0))
    return pl.pallas_call(
        functools.partial(_ffn_kernel, n_x=len(xs), fc=512),
        grid=(e, b // bb),
        in_specs=[xspec(x) for x in xs] + [
            pl.BlockSpec((1, 1, d, f), lambda ee, j: (layer, ee, 0, 0)),
            pl.BlockSpec((1, 1, d, f), lambda ee, j: (layer, ee, 0, 0)),
            pl.BlockSpec((1, 1, f, d), lambda ee, j: (layer, ee, 0, 0)),
        ],
        out_specs=[xspec(x) for x in xs],
        out_shape=[jax.ShapeDtypeStruct(x.shape, BF16) for x in xs],
        scratch_shapes=[pltpu.VMEM((d, f), BF16), pltpu.VMEM((d, f), BF16), pltpu.VMEM((f, d), BF16)],
        compiler_params=_cp("arbitrary", "arbitrary", vmem=56 * 1024 * 1024),
        name=name,
    )(*xs, w_gate, w_up, w_down)


def _combine_kernel(st_ref, ye_ref, post_ref, ptf_ref, rep_ref, aff_ref, h_ref, mod_ref, o_ref,
                    *, cap, tc, win, n_chunks):
    b = pl.program_id(0)
    n_e = ye_ref.shape[1]
    d = ye_ref.shape[3]
    t = h_ref.shape[1]
    gate = mod_ref[0, 0][5:6]
    for cc in range(t // tc):
        c = pl.program_id(1) * (t // tc) + cc
        rows = slice(cc * tc, (cc + 1) * tc)
        pt = post_ref[0, rows, :]
        at = aff_ref[0, rows, :]
        starts, fits = _windows(st_ref, b, c, n_e, n_chunks, cap, win)

        def finish(w, y):
            o_ref[0, rows, :] = h_ref[0, rows, :] + gate * _dot(w, y)

        @pl.when(fits)
        def _():
            y = jnp.concatenate([ye_ref[0, e, pl.ds(starts[e], win), :] for e in range(n_e)], axis=0)
            rep = rep_ref[...]
            slot_of = _dot(ptf_ref[0, rows, :], rep)
            gate_of = _dot(at.astype(BF16), rep)
            lane = lax.broadcasted_iota(jnp.int32, (1, n_e * win), 1)
            want = lane % win
            for e in range(n_e):
                want = jnp.where(lane // win == e, want + starts[e], want)
            w = jnp.where(slot_of == want.astype(F32), gate_of, 0.0).astype(BF16)
            finish(w, y)

        @pl.when(jnp.logical_not(fits))
        def _():
            y = ye_ref[0].reshape(n_e * cap, d)
            if cap % LANE == 0:
                col = lax.broadcasted_iota(jnp.int32, (tc, cap), 1)
                w = jnp.concatenate(
                    [jnp.where(pt[:, e:e + 1] == col, at[:, e:e + 1], 0.0).astype(BF16) for e in range(n_e)], axis=1)
            else:
                col = lax.broadcasted_iota(jnp.int32, (tc, n_e * cap), 1)
                acc = jnp.zeros((tc, n_e * cap), F32)
                for e in range(n_e):
                    slot = pt[:, e:e + 1]
                    key = jnp.where(slot >= 0, slot + e * cap, -1)
                    acc = jnp.where(key == col, at[:, e:e + 1], acc)
                w = acc.astype(BF16)
            finish(w, y)


def _combine(ye, pos_t, aff, starts, h2, mod, layer, mod_row, tc, win, *, name):
    b, l, d = h2.shape
    e, cap = ye.shape[1], ye.shape[2]
    t = min(512, l)
    mrow = (lambda bb: bb) if mod_row is None else (lambda bb: mod_row)
    assert cap <= 256 and LANE % win == 0
    pos_tf = jnp.pad(pos_t.astype(BF16), ((0, 0), (0, 0), (0, LANE - e)))
    rep = (jnp.arange(LANE)[:, None] == jnp.arange(e * win)[None, :] // win).astype(BF16)
    return pl.pallas_call(
        functools.partial(_combine_kernel, cap=cap, tc=tc, win=win, n_chunks=l // tc),
        grid_spec=pltpu.PrefetchScalarGridSpec(
            num_scalar_prefetch=1,
            grid=(b, l // t),
            in_specs=[
                pl.BlockSpec((1, e, cap, d), lambda bb, i, st: (bb, 0, 0, 0)),
                pl.BlockSpec((1, t, e), lambda bb, i, st: (bb, i, 0)),
                pl.BlockSpec((1, t, LANE), lambda bb, i, st: (bb, i, 0)),
                pl.BlockSpec((LANE, e * win), lambda bb, i, st: (0, 0)),
                pl.BlockSpec((1, t, LANE), lambda bb, i, st: (bb, i, 0)),
                pl.BlockSpec((1, t, d), lambda bb, i, st: (bb, i, 0)),
                pl.BlockSpec((1, 1, 6, d), lambda bb, i, st: (layer, mrow(bb), 0, 0)),
            ],
            out_specs=pl.BlockSpec((1, t, d), lambda bb, i, st: (bb, i, 0)),
        ),
        out_shape=jax.ShapeDtypeStruct((b, l, d), F32),
        compiler_params=_cp("arbitrary", "arbitrary"),
        name=name,
    )(starts, ye, pos_t, pos_tf, rep, aff, h2, mod)


_PERM64 = tuple(list(range(0, 16)) + list(range(32, 48)) + list(range(16, 32)) + list(range(48, 64)))


def _rope_tables(seq_len):
    t = jnp.arange(seq_len)
    row = (t // GRID_W).astype(F32)
    col = (t % GRID_W).astype(F32)
    n_freq = C_HEAD_DIM // 4
    inv = ROPE_THETA ** (-jnp.arange(n_freq, dtype=F32) / n_freq)
    ar, ac = row[:, None] * inv[None], col[:, None] * inv[None]
    c64 = jnp.concatenate([jnp.cos(ar), jnp.cos(ac), jnp.cos(ar), jnp.cos(ac)], axis=-1)
    s64 = jnp.concatenate([-jnp.sin(ar), -jnp.sin(ac), jnp.sin(ar), jnp.sin(ac)], axis=-1)
    reps = C_QK // C_HEAD_DIM
    return jnp.tile(c64, (1, reps)), jnp.tile(s64, (1, reps))


def kernel(x, c, ctx, c_ctx, w_ada, b_ada, g_mix, g_ffn, w_in_ab, conv_a, conv_b, conv_b_bias, ln_b_g, ln_b_b, w_out_ab, w_in_cd, g_q, g_k, lam_q1, lam_k1, lam_q2, lam_k2, g_subln, conv_d, hf_w1, hf_b1, hf_freq, hf_w2, hf_b2, hf_w3, hf_bias, w_out_cd, w_router, w_gate, w_up, w_down):
    bsz, seq_lat, d = x.shape
    seq_ctx = ctx.shape[1]
    cap_lat = max(1, (EC_CAPACITY * seq_lat) // N_EXPERTS)
    cap_ctx = max(1, (EC_CAPACITY * seq_ctx) // N_EXPERTS)

    cond = jnp.zeros((MOD_ROWS, d), F32).at[:bsz].set(c).at[CTX_ROW].set(c_ctx)
    mod = _ada(cond, w_ada, b_ada)

    perm = jnp.asarray(_PERM64)

    def permute_qk(w):
        quarter = C_HEAD_DIM // 4
        qk = w[:, :2 * C_QK].reshape(d, 2 * C_QK // C_HEAD_DIM, 2, 2, quarter)
        return jnp.concatenate([jnp.swapaxes(qk, 2, 3).reshape(d, 2 * C_QK), w[:, 2 * C_QK:]], axis=1)

    gi = jnp.arange(C_QK) // C_HEAD_DIM
    gsum = (gi[:, None] == gi[None, :]).astype(BF16)
    cos_t, sin_t = _rope_tables(seq_lat)
    dft_lat, dft_ctx = _dft_mats(seq_lat), _dft_mats(seq_ctx)

    h_lat, h_ctx = x, ctx
    for l in range(DEPTH):
        last = l == DEPTH - 1
        odd = l % 2 == 1
        i = l // 2
        run_ctx = odd or not last
        gm = g_mix[l][None]
        if not odd:
            w_in = w_in_ab[i].astype(BF16)
            w_out = w_out_ab[i].astype(BF16)
            cargs = (conv_a[i], conv_b[i], conv_b_bias[i], ln_b_g[i], ln_b_b[i])
            p_lat = _proj_in(h_lat, mod, l, None, gm, w_in, name=f"proj_in_lat{l}")
            ys_lat = [_conv_mix(p_lat, *cargs, name=f"conv_mix_lat{l}")]
            if not last:
                p_ctx = _proj_in(h_ctx, mod, l, CTX_ROW, gm, w_in, name=f"proj_in_ctx{l}")
                ys_ctx = [_conv_mix(p_ctx, *cargs, name=f"conv_mix_ctx{l}")]
        else:
            lam_init = 0.8 - 0.6 * math.exp(-0.3 * l)
            w_in = permute_qk(w_in_cd[i]).astype(BF16)
            w_out = w_out_cd[i].astype(BF16)
            gqk = jnp.stack([jnp.tile(g_q[i][perm], C_QK // C_HEAD_DIM) * (C_HEAD_DIM ** -0.5 * math.log2(math.e)),
                             jnp.tile(g_k[i][perm], C_QK // C_HEAD_DIM)])
            lamv = jnp.stack([lam_q1[i], lam_k1[i], lam_q2[i], lam_k2[i]])
            gsub = g_subln[i][None]
            p_lat = _proj_in(h_lat, mod, l, None, gm, w_in, norm_rows=(0, 1), gqk=gqk, gsum=gsum,
                             cos=cos_t, sin=sin_t, name=f"proj_in_lat{l}")
            if last:
                p_ctx = _proj_in(h_ctx, mod, l, CTX_ROW, gm, w_in[:, C_QK:2 * C_QK + C_V], norm_rows=(1,),
                                 gqk=gqk, gsum=gsum, name=f"proj_in_ctx{l}")
                kv_off = 0
            else:
                p_ctx = _proj_in(h_ctx, mod, l, CTX_ROW, gm, w_in, norm_rows=(0, 1), gqk=gqk, gsum=gsum,
                                 name=f"proj_in_ctx{l}")
                kv_off = C_HEADS
            o_lat = _attn(p_lat, p_ctx, kv_off, p_lat, lamv, gsub, lam_init, name=f"attn_lat{l}")
            fargs = (hf_w1[i], hf_b1[i], hf_freq[i], hf_w2[i], hf_b2[i], hf_w3[i])
            filt_lat = _hy_filter(seq_lat, *fargs, name=f"hy_filter_lat{l}")
            ys_lat = [o_lat, _hyena(p_lat, conv_d[i], filt_lat, hf_bias[i], dft_lat, f"lat{l}")]
            if not last:
                o_ctx = _attn(p_ctx, p_ctx, kv_off, None, lamv, gsub, lam_init, name=f"attn_ctx{l}")
                filt_ctx = _hy_filter(seq_ctx, *fargs, name=f"hy_filter_ctx{l}")
                ys_ctx = [o_ctx, _hyena(p_ctx, conv_d[i], filt_ctx, hf_bias[i], dft_ctx, f"ctx{l}")]

        gf = g_ffn[l][None]
        wr = jnp.pad(w_router[l], ((0, 0), (0, LANE - N_EXPERTS)))
        streams = [("lat", h_lat, ys_lat, None, cap_lat)]
        if not last:
            streams.append(("ctx", h_ctx, ys_ctx, CTX_ROW, cap_ctx))
        staged = []
        for tag, h, ys, mrow, cap in streams:
            h2, u2, aff, afft = _proj_out(ys, w_out, h, mod, l, mrow, gf, wr, name=f"proj_out_{tag}{l}")
            tc = min(MOE_CHUNK, h.shape[1])
            win = min(MOE_WINDOW, cap)
            pos, starts = _route(afft, cap, tc, name=f"route_{tag}{l}")
            xe = _gather(u2, pos, starts, cap, tc, win, name=f"gather_{tag}{l}")
            staged.append((tag, h2, aff, pos, starts, xe, mrow, tc, win))
        yes = _ffn([s[5] for s in staged], w_gate, w_up, w_down, l, name=f"ffn{l}")
        outs = []
        for (tag, h2, aff, pos, starts, _, mrow, tc, win), ye in zip(staged, yes):
            pos_t = jnp.transpose(pos, (0, 2, 1))
            outs.append(_combine(ye, pos_t, aff, starts, h2, mod, l, mrow, tc, win, name=f"combine_{tag}{l}"))
        h_lat = outs[0]
        if not last:
            h_ctx = outs[1]
    return h_lat
```

```python
import functools
import math

import jax
import jax.numpy as jnp
from jax import lax
from jax.experimental import pallas as pl
from jax.experimental.pallas import tpu as pltpu

F32 = jnp.float32
BF16 = jnp.bfloat16

D_MODEL = 1024
DEPTH = 4
GRID_W = 64
A_WIDTH = 512
B_WIDTH = 512
SHORT_CONV_W = 3
CONFORMER_CONV_W = 31
AB_IN = 3 * A_WIDTH + 2 * B_WIDTH
C_HEAD_DIM = 64
C_V_DIM = 128
C_HEADS = 4
C_QK = 512
C_V = 512
HY_WIDTH = 512
CD_IN = 2 * C_QK + C_V + 3 * HY_WIDTH
ROPE_THETA = 10000.0
HY_BANDS = 16
HY_EMB = 2 * HY_BANDS + 1
HY_MIN_DECAY = math.log(1e-2) / 1.5
HY_MAX_DECAY = math.log(1e-2) / 0.3
N_EXPERTS = 16
EXPERT_FF = 1024
EC_CAPACITY = 2
NORM_EPS = 1e-6
SUBLN_EPS = 1e-5

LANE = 128
SUBLANE = 8
HALO = 16
CONV_ROWS = 128
ROW_ALIGN = 16
MOE_CHUNK = 256
MOE_WINDOW = 64
ATTN_ROWS = 128
ROW_CHUNK = 256
MOD_ROWS = 16
CTX_ROW = 8
VMEM_LIMIT = 52 * 1024 * 1024
F32_MIN_NORMAL_BITS = 0x00800000


def _cp(*sem, vmem=VMEM_LIMIT):
    return pltpu.CompilerParams(dimension_semantics=sem, vmem_limit_bytes=vmem)


def _dot(a, b):
    return jnp.dot(a, b, preferred_element_type=F32)


def _dot3(a, b):
    a_hi = a.astype(BF16)
    a_lo = (a - a_hi.astype(F32)).astype(BF16)
    b_hi = b.astype(BF16)
    b_lo = (b - b_hi.astype(F32)).astype(BF16)
    return _dot(a_hi, b_hi) + (_dot(a_hi, b_lo) + _dot(a_lo, b_hi))


def _rms_mod(h, g, shift, scale):
    ms = jnp.mean(h * h, axis=-1, keepdims=True)
    return (h * lax.rsqrt(ms + NORM_EPS) * g) * (1.0 + scale) + shift


def _ada_kernel(c_ref, w_ref, b_ref, o_ref):
    c = c_ref[...]
    s = (c * jax.nn.sigmoid(c)).astype(BF16)
    o_ref[0] = _dot(s, w_ref[0].astype(BF16)) + b_ref[0]


def _ada(cond, w_ada, b_ada):
    depth, d, n = w_ada.shape
    tn = 1536
    out = pl.pallas_call(
        _ada_kernel,
        grid=(depth, n // tn),
        in_specs=[
            pl.BlockSpec((MOD_ROWS, d), lambda l, j: (0, 0)),
            pl.BlockSpec((1, d, tn), lambda l, j: (l, 0, j)),
            pl.BlockSpec((1, 1, tn), lambda l, j: (l, 0, j)),
        ],
        out_specs=pl.BlockSpec((1, MOD_ROWS, tn), lambda l, j: (l, 0, j)),
        out_shape=jax.ShapeDtypeStruct((depth, MOD_ROWS, n), F32),
        compiler_params=_cp("arbitrary", "arbitrary"),
        name="ada",
    )(cond, w_ada, b_ada.reshape(depth, 1, n))
    return out.reshape(depth, MOD_ROWS, 6, d)


def _swap32(x):
    n = x.shape[-1]
    lane = lax.broadcasted_iota(jnp.int32, x.shape, x.ndim - 1)
    up = pltpu.roll(x, n - 32, x.ndim - 1)
    dn = pltpu.roll(x, 32, x.ndim - 1)
    return jnp.where((lane & 32) == 0, up, dn)


def _proj_in_kernel(*refs, n_out, chunk, norm_rows, rope):
    h_ref, mod_ref, g_ref, w_ref = refs[:4]
    k = 4
    if norm_rows:
        gqk_ref, gsum_ref = refs[k:k + 2]
        k += 2
    if rope:
        cos_ref, sin_ref = refs[k:k + 2]
        k += 2
    o_ref, u_scr = refs[k], refs[k + 1]
    mod = mod_ref[0, 0]
    u = _rms_mod(h_ref[0], g_ref[...], mod[0:1], mod[1:2])
    u_scr[...] = u.astype(BF16)
    for n in range(n_out // chunk):
        p = _dot(u_scr[...], w_ref[:, n * chunk:(n + 1) * chunk])
        if n < len(norm_rows):
            ss = _dot((p * p).astype(BF16), gsum_ref[...])
            r = norm_rows[n]
            p = p * lax.rsqrt(ss * (1.0 / C_HEAD_DIM) + NORM_EPS) * gqk_ref[r:r + 1, :]
            if rope:
                p = p * cos_ref[...] + _swap32(p) * sin_ref[...]
        o_ref[0, :, n * chunk:(n + 1) * chunk] = p.astype(o_ref.dtype)


def _proj_in(h, mod, layer, mod_row, g, w, *, norm_rows=(), gqk=None, gsum=None, cos=None, sin=None, name):
    b, l, d = h.shape
    n_out = w.shape[1]
    t = min(1024, l)
    chunk = 512
    rope = cos is not None
    mrow = (lambda bb: bb) if mod_row is None else (lambda bb: mod_row)
    in_specs = [
        pl.BlockSpec((1, t, d), lambda i, bb: (bb, i, 0)),
        pl.BlockSpec((1, 1, 6, d), lambda i, bb: (layer, mrow(bb), 0, 0)),
        pl.BlockSpec((1, d), lambda i, bb: (0, 0)),
        pl.BlockSpec((d, n_out), lambda i, bb: (0, 0)),
    ]
    args = [h, mod, g, w]
    if norm_rows:
        in_specs += [pl.BlockSpec(gqk.shape, lambda i, bb: (0, 0)),
                     pl.BlockSpec(gsum.shape, lambda i, bb: (0, 0))]
        args += [gqk, gsum]
    if rope:
        in_specs += [pl.BlockSpec((t, chunk), lambda i, bb: (i, 0))] * 2
        args += [cos, sin]
    return pl.pallas_call(
        functools.partial(_proj_in_kernel, n_out=n_out, chunk=chunk, norm_rows=tuple(norm_rows), rope=rope),
        grid=(l // t, b),
        in_specs=in_specs,
        out_specs=pl.BlockSpec((1, t, n_out), lambda i, bb: (bb, i, 0)),
        out_shape=jax.ShapeDtypeStruct((b, l, n_out), BF16),
        scratch_shapes=[pltpu.VMEM((t, d), BF16)],
        compiler_params=_cp("arbitrary", "arbitrary"),
        name=name,
    )(*args)


def _halo_specs(t, l, width, col):
    r = t // HALO
    last = l // HALO - 1
    return [
        pl.BlockSpec((1, t, width), lambda bb, i: (bb, i, col)),
        pl.BlockSpec((1, HALO, width), lambda bb, i: (bb, jnp.maximum(i * r - 1, 0), col)),
        pl.BlockSpec((1, HALO, width), lambda bb, i: (bb, jnp.minimum((i + 1) * r, last), col)),
    ]


def _conv_mix_kernel(pc_ref, pp_ref, pn_ref, ca_ref, cb_ref, bb_ref, lg_ref, lb_ref, o_ref,
                     abuf, gbuf, zbuf, sbuf, wbuf, *, t, nt):
    i = pl.program_id(1)
    keep_prev = jnp.where(i == 0, 0.0, 1.0)
    keep_next = jnp.where(i == nt - 1, 0.0, 1.0)

    def prods(p):
        p = p.astype(F32)
        a = p[:, A_WIDTH:2 * A_WIDTH] * p[:, 2 * A_WIDTH:3 * A_WIDTH]
        g = p[:, 3 * A_WIDTH:3 * A_WIDTH + B_WIDTH] * jax.nn.sigmoid(p[:, 3 * A_WIDTH + B_WIDTH:])
        return a, g

    n_slab = B_WIDTH // LANE

    def put_g(r0, n, g):
        for c in range(n_slab):
            gbuf[c, r0:r0 + n, :] = g[:, c * LANE:(c + 1) * LANE]

    a, g = prods(pp_ref[0])
    abuf[0:HALO] = a * keep_prev
    put_g(0, HALO, g * keep_prev)
    a, g = prods(pc_ref[0])
    abuf[HALO:HALO + t] = a
    put_g(HALO, t, g)
    a, g = prods(pn_ref[0])
    abuf[HALO + t:2 * HALO + t] = a * keep_next
    put_g(HALO + t, HALO, g * keep_next)

    ca = ca_ref[...]
    ya = (ca[0:1] * abuf[HALO - 1:HALO - 1 + t] + ca[1:2] * abuf[HALO:HALO + t]
          + ca[2:3] * abuf[HALO + 1:HALO + 1 + t])
    o_ref[0, :, 0:A_WIDTH] = (pc_ref[0, :, 0:A_WIDTH].astype(F32) * ya).astype(o_ref.dtype)

    pad = (CONFORMER_CONV_W - 1) // 2
    rb = CONV_ROWS
    nv = rb // SUBLANE
    tail = gbuf.shape[1] - 2 * HALO - t
    put_g(2 * HALO + t, tail, jnp.zeros((tail, B_WIDTH), F32))

    def shift_rows(i, carry):
        base = pl.multiple_of(i * rb, rb)
        for c in range(n_slab):
            x = gbuf[c, pl.ds(base, rb + SUBLANE), :]
            for r in range(1, SUBLANE):
                sbuf[r - 1, c, pl.ds(base, rb), :] = x[r:r + rb]
        return carry

    lax.fori_loop(0, pl.cdiv(t + 2 * HALO - SUBLANE, rb), shift_rows, 0)
    for k in range(CONFORMER_CONV_W):
        wbuf[k] = jnp.broadcast_to(cb_ref[k:k + 1, :], (SUBLANE, B_WIDTH))

    def conv_rows(i, carry):
        base = pl.multiple_of(i * rb, rb)
        for c in range(n_slab):
            c0 = c * LANE
            acc = jnp.zeros((nv, SUBLANE, LANE), F32) + bb_ref[:, c0:c0 + LANE]
            for k in range(CONFORMER_CONV_W):
                s = HALO - pad + k
                r = s % SUBLANE
                src = gbuf.at[c] if r == 0 else sbuf.at[r - 1, c]
                rows = src[pl.ds(base + (s - r), rb), :].reshape(nv, SUBLANE, LANE)
                acc = acc + wbuf[k, :, c0:c0 + LANE] * rows
            zbuf[pl.ds(base, rb), c0:c0 + LANE] = acc.reshape(rb, LANE)
        return carry

    lax.fori_loop(0, t // rb, conv_rows, 0)
    z = zbuf[...]
    mu = jnp.mean(z, axis=-1, keepdims=True)
    zc = z - mu
    var = jnp.mean(zc * zc, axis=-1, keepdims=True)
    y = zc * lax.rsqrt(var + NORM_EPS) * lg_ref[...] + lb_ref[...]
    o_ref[0, :, A_WIDTH:] = (y * jax.nn.sigmoid(y)).astype(o_ref.dtype)


def _conv_mix(p, conv_a, conv_b, conv_b_bias, ln_g, ln_b, *, name):
    b, l, n = p.shape
    t = min(512, l)
    nt = l // t
    full = lambda shape: pl.BlockSpec(shape, lambda bb, i: (0,) * len(shape))
    return pl.pallas_call(
        functools.partial(_conv_mix_kernel, t=t, nt=nt),
        grid=(b, nt),
        in_specs=_halo_specs(t, l, n, 0) + [
            full(conv_a.shape), full(conv_b.shape), full((1, B_WIDTH)), full((1, B_WIDTH)), full((1, B_WIDTH))],
        out_specs=pl.BlockSpec((1, t, A_WIDTH + B_WIDTH), lambda bb, i: (bb, i, 0)),
        out_shape=jax.ShapeDtypeStruct((b, l, A_WIDTH + B_WIDTH), BF16),
        scratch_shapes=[pltpu.VMEM((t + 2 * HALO, A_WIDTH), F32),
                        pltpu.VMEM((B_WIDTH // LANE, t + 2 * HALO + CONV_ROWS, LANE), F32),
                        pltpu.VMEM((t, B_WIDTH), F32),
                        pltpu.VMEM((SUBLANE - 1, B_WIDTH // LANE, t + 2 * HALO + CONV_ROWS, LANE), F32),
                        pltpu.VMEM((CONFORMER_CONV_W, SUBLANE, B_WIDTH), F32)],
        compiler_params=_cp("arbitrary", "arbitrary"),
        name=name,
    )(p, p, p, conv_a, conv_b, conv_b_bias[None], ln_g[None], ln_b[None])


def _attn_kernel(*refs, lam_init, has_lat):
    if has_lat:
        q_ref, kc_ref, vc_ref, kl_ref, vl_ref, lam_ref, gs_ref, o_ref = refs
    else:
        q_ref, kc_ref, vc_ref, lam_ref, gs_ref, o_ref = refs
    lv = lam_ref[...]
    lam = (jnp.exp(jnp.sum(lv[0:1] * lv[1:2], axis=-1, keepdims=True))
           - jnp.exp(jnp.sum(lv[2:3] * lv[3:4], axis=-1, keepdims=True)) + lam_init)
    q = q_ref[0]
    lane = lax.broadcasted_iota(jnp.int32, q.shape, 1)
    nt = (((1,), (1,)), ((), ()))
    ones_col = lambda v: jnp.concatenate([v, jnp.ones_like(v)], axis=1)
    vc = ones_col(vc_ref[0])
    if has_lat:
        vl = ones_col(vl_ref[0])
    halves = []
    tq = q.shape[0]
    rc = min(ATTN_ROWS, tq)
    for j in range(2):
        qj = jnp.where((lane >= C_HEAD_DIM) == (j == 1), q, jnp.zeros_like(q))
        outs = []
        for r0 in range(0, tq, rc):
            qr = qj[r0:r0 + rc]
            s_c = lax.dot_general(qr, kc_ref[0], nt, preferred_element_type=F32)
            m = jnp.max(s_c, axis=-1, keepdims=True)
            if has_lat:
                s_l = lax.dot_general(qr, kl_ref[0], nt, preferred_element_type=F32)
                m = jnp.maximum(m, jnp.max(s_l, axis=-1, keepdims=True))
            oj = _dot(jnp.exp2(s_c - m).astype(BF16), vc)
            if has_lat:
                oj = oj + _dot(jnp.exp2(s_l - m).astype(BF16), vl)
            outs.append(oj[:, 0:C_V_DIM] * (1.0 / oj[:, C_V_DIM:C_V_DIM + 1]))
        halves.append(jnp.concatenate(outs, axis=0))
    o = halves[0] - lam * halves[1]
    ms = jnp.mean(o * o, axis=-1, keepdims=True)
    o = o * lax.rsqrt(ms + SUBLN_EPS) * gs_ref[...] * (1.0 - lam_init)
    o_ref[0] = o.astype(o_ref.dtype)


def _attn(pq, pkv_ctx, kv_off, p_lat, lamv, g_sub, lam_init, *, name):
    b, lq, _ = pq.shape
    lc = pkv_ctx.shape[1]
    tq = min(1024, lq)
    has_lat = p_lat is not None
    in_specs = [
        pl.BlockSpec((1, tq, LANE), lambda bb, h, i: (bb, i, h)),
        pl.BlockSpec((1, lc, LANE), lambda bb, h, i: (bb, 0, kv_off + h)),
        pl.BlockSpec((1, lc, LANE), lambda bb, h, i: (bb, 0, kv_off + C_HEADS + h)),
    ]
    args = [pq, pkv_ctx, pkv_ctx]
    if has_lat:
        ll = p_lat.shape[1]
        in_specs += [pl.BlockSpec((1, ll, LANE), lambda bb, h, i: (bb, 0, C_HEADS + h)),
                     pl.BlockSpec((1, ll, LANE), lambda bb, h, i: (bb, 0, 2 * C_HEADS + h))]
        args += [p_lat, p_lat]
    in_specs += [pl.BlockSpec((4, C_HEAD_DIM), lambda bb, h, i: (0, 0)),
                 pl.BlockSpec((1, C_V_DIM), lambda bb, h, i: (0, 0))]
    args += [lamv, g_sub]
    return pl.pallas_call(
        functools.partial(_attn_kernel, lam_init=lam_init, has_lat=has_lat),
        grid=(b, C_HEADS, lq // tq),
        in_specs=in_specs,
        out_specs=pl.BlockSpec((1, tq, LANE), lambda bb, h, i: (bb, i, h)),
        out_shape=jax.ShapeDtypeStruct((b, lq, C_V), BF16),
        compiler_params=_cp("arbitrary", "arbitrary", "arbitrary"),
        name=name,
    )(*args)


def _hy_pre_kernel(pc_ref, pp_ref, pn_ref, cd_ref, o_ref, buf, *, t, nt):
    i = pl.program_id(1)
    buf[0:HALO] = pp_ref[0].astype(F32) * jnp.where(i == 0, 0.0, 1.0)
    buf[HALO:HALO + t] = pc_ref[0].astype(F32)
    buf[HALO + t:2 * HALO + t] = pn_ref[0].astype(F32) * jnp.where(i == nt - 1, 0.0, 1.0)
    cd = cd_ref[...]
    z = (cd[0:1] * buf[HALO - 1:HALO - 1 + t] + cd[1:2] * buf[HALO:HALO + t]
         + cd[2:3] * buf[HALO + 1:HALO + 1 + t])
    o_ref[0, :, 0:HY_WIDTH] = z[:, 0:HY_WIDTH].astype(o_ref.dtype)
    o_ref[0, :, HY_WIDTH:] = (z[:, HY_WIDTH:2 * HY_WIDTH] * z[:, 2 * HY_WIDTH:]).astype(o_ref.dtype)


def _hy_pre(p, conv_d, *, name):
    b, l, _ = p.shape
    t = min(512, l)
    nt = l // t
    w = 3 * HY_WIDTH
    return pl.pallas_call(
        functools.partial(_hy_pre_kernel, t=t, nt=nt),
        grid=(b, nt),
        in_specs=_halo_specs(t, l, w, 1) + [pl.BlockSpec(conv_d.shape, lambda bb, i: (0, 0))],
        out_specs=pl.BlockSpec((1, t, 2 * HY_WIDTH), lambda bb, i: (bb, i, 0)),
        out_shape=jax.ShapeDtypeStruct((b, l, 2 * HY_WIDTH), BF16),
        scratch_shapes=[pltpu.VMEM((t + 2 * HALO, w), F32)],
        compiler_params=_cp("arbitrary", "arbitrary"),
        name=name,
    )(p, p, p, conv_d)


def _hy_filter_kernel(z_ref, w1_ref, b1_ref, fr_ref, w2_ref, b2_ref, w3_ref, dist_ref, dec_ref, o_ref):
    fr = fr_ref[...]
    hid = jnp.sin(fr * (_dot3(z_ref[...], w1_ref[...]) + b1_ref[...]))
    hid = jnp.sin(fr * (_dot3(hid, w2_ref[...]) + b2_ref[...]))
    h = _dot3(hid, w3_ref[...])
    o_ref[...] = h * jnp.exp(-dist_ref[...] * dec_ref[...])


def _hy_filter(seq_len, w1, b1, freq, w2, b2, w3, *, name):
    t = jnp.arange(seq_len, dtype=F32)
    t_unit = jnp.linspace(0.0, 1.0, seq_len)[:, None]
    bands = jnp.linspace(1e-4, HY_BANDS - 1, HY_BANDS)
    ang = (2.0 * math.pi / seq_len) * t[:, None] * bands[None]
    z = jnp.concatenate([t_unit, jnp.cos(ang), -jnp.sin(ang)], axis=-1)
    centre = seq_len // 2
    dist = (jnp.abs(t - centre) / max(centre, 1))[:, None]
    decay = jnp.abs(jnp.linspace(HY_MIN_DECAY, HY_MAX_DECAY, HY_WIDTH))[None]
    pad_c = lambda a, n: jnp.pad(a, ((0, 0), (0, n - a.shape[1])))
    pad_r = lambda a, n: jnp.pad(a, ((0, n - a.shape[0]), (0, 0)))
    args = [pad_c(z, LANE), pad_c(pad_r(w1, LANE), LANE), pad_c(b1[None], LANE), pad_c(freq[None], LANE),
            pad_c(pad_r(w2, LANE), LANE), pad_c(b2[None], LANE), pad_r(w3, LANE), dist, decay]
    return pl.pallas_call(
        _hy_filter_kernel,
        out_shape=jax.ShapeDtypeStruct((seq_len, HY_WIDTH), F32),
        compiler_params=pltpu.CompilerParams(vmem_limit_bytes=VMEM_LIMIT),
        name=name,
    )(*args)


def _dft_mats(l):
    m = 3 * l // 2
    hh = m // 2
    f = jnp.arange(hh, dtype=jnp.int32)
    s = jnp.arange(l, dtype=jnp.int32)
    n = s + l // 2

    def cos_sin(rows, n_cols):
        w = 64
        phase = lambda k: (2.0 * math.pi / m) * ((rows[:, None] * k[None, :]) % m).astype(F32)
        a = phase(jnp.arange(0, n_cols, w, dtype=jnp.int32))[:, :, None]
        b = phase(jnp.arange(w, dtype=jnp.int32))[:, None, :]
        ca, sa, cb, sb = jnp.cos(a), jnp.sin(a), jnp.cos(b), jnp.sin(b)
        shape = (rows.shape[0], n_cols)
        return (ca * cb - sa * sb).reshape(shape), (sa * cb + ca * sb).reshape(shape)

    c, sn = cos_sin(f, l)
    nyq = jnp.where(s % 2 == 0, 1.0, -1.0).astype(F32)
    fw = jnp.stack([c, jnp.where(f[:, None] == 0, nyq[None, :], -sn)]).astype(BF16)
    c, sn = cos_sin(n, hh)
    nyq_n = jnp.where(n % 2 == 0, 1.0, -1.0).astype(F32)
    iv_r = jnp.where(f[None, :] == 0, 1.0, 2.0 * c)
    iv_i = jnp.where(f[None, :] == 0, nyq_n[:, None], -2.0 * sn)
    iv = jnp.concatenate([iv_r, iv_i], axis=1).astype(BF16)
    return fw, iv


def _dft_fwd_kernel(fw_ref, v_ref, h_ref, o_ref, *, tf):
    v = v_ref[0].astype(BF16)
    ur = _dot(fw_ref[0], v)
    ui = _dot(fw_ref[1], v)
    hr = h_ref[0]
    hi = h_ref[1]
    row = lax.broadcasted_iota(jnp.int32, ur.shape, 0) + pl.program_id(0) * tf
    packed = row == 0
    uihi = ui * hi
    o_ref[0, 0] = (ur * hr - jnp.where(packed, 0.0, uihi)).astype(o_ref.dtype)
    o_ref[0, 1] = jnp.where(packed, uihi, ur * hi + ui * hr).astype(o_ref.dtype)


def _dft_fwd(fw, v, col, hspec, out_dtype, *, name):
    b, l, _ = v.shape
    hh = fw.shape[1]
    tf = min(512, hh)
    return pl.pallas_call(
        functools.partial(_dft_fwd_kernel, tf=tf),
        grid=(hh // tf, b),
        in_specs=[
            pl.BlockSpec((2, tf, l), lambda i, bb: (0, i, 0)),
            pl.BlockSpec((1, l, HY_WIDTH), lambda i, bb: (bb, 0, col)),
            pl.BlockSpec((2, tf, HY_WIDTH), lambda i, bb: (0, i, 0)),
        ],
        out_specs=pl.BlockSpec((1, 2, tf, HY_WIDTH), lambda i, bb: (bb, 0, i, 0)),
        out_shape=jax.ShapeDtypeStruct((b, 2, hh, HY_WIDTH), out_dtype),
        compiler_params=_cp("arbitrary", "arbitrary"),
        name=name,
    )(fw, v, hspec)


def _dft_inv_kernel(iv_ref, y_ref, gv_ref, bias_ref, o_ref, *, inv_m):
    m = iv_ref.shape[1]
    yhat = y_ref[0].reshape(m, HY_WIDTH)
    conv = _dot(iv_ref[...], yhat) * inv_m
    gv = gv_ref[0].astype(F32)
    o_ref[0] = (gv[:, 0:HY_WIDTH] * (conv + gv[:, HY_WIDTH:] * bias_ref[...])).astype(o_ref.dtype)


def _dft_inv(iv, yhat, gv, bias, *, name):
    b, l, _ = gv.shape
    m = iv.shape[1]
    tn = min(512, l)
    return pl.pallas_call(
        functools.partial(_dft_inv_kernel, inv_m=1.0 / m),
        grid=(l // tn, b),
        in_specs=[
            pl.BlockSpec((tn, m), lambda i, bb: (i, 0)),
            pl.BlockSpec((1, 2, m // 2, HY_WIDTH), lambda i, bb: (bb, 0, 0, 0)),
            pl.BlockSpec((1, tn, 2 * HY_WIDTH), lambda i, bb: (bb, i, 0)),
            pl.BlockSpec((1, HY_WIDTH), lambda i, bb: (0, 0)),
        ],
        out_specs=pl.BlockSpec((1, tn, HY_WIDTH), lambda i, bb: (bb, i, 0)),
        out_shape=jax.ShapeDtypeStruct((b, l, HY_WIDTH), BF16),
        compiler_params=_cp("arbitrary", "arbitrary"),
        name=name,
    )(iv, yhat, gv, bias)


def _hyena(p, conv_d, filt, bias, dft, tag):
    fw, iv = dft
    hh = fw.shape[1]
    ident = jnp.stack([jnp.ones((hh, HY_WIDTH), F32),
                       jnp.zeros((hh, HY_WIDTH), F32).at[0].set(1.0)])
    fhat = _dft_fwd(fw, filt[None], 0, ident, F32, name="hy_fhat_" + tag)[0]
    gv = _hy_pre(p, conv_d, name="hy_pre_" + tag)
    yhat = _dft_fwd(fw, gv, 1, fhat, BF16, name="hy_fwd_" + tag)
    return _dft_inv(iv, yhat, gv, bias[None], name="hy_inv_" + tag)


def _proj_out_kernel(*refs, n_y):
    y_refs = refs[:n_y]
    w_ref, h_ref, mod_ref, g_ref, wr_ref, h2_ref, u2_ref, aff_ref, afft_ref = refs[n_y:]
    mod = mod_ref[0, 0]
    wr = wr_ref[...]
    wr_hi = wr.astype(BF16).astype(F32)
    wr_hl = (wr_hi + pltpu.roll(wr - wr_hi, N_EXPERTS, 1)).astype(BF16)
    t = h_ref.shape[1]
    rc = min(ROW_CHUNK, t)
    for r0 in range(0, t, rc):
        acc = None
        k0 = 0
        for y_ref in y_refs:
            kk = y_ref.shape[2]
            part = _dot(y_ref[0, r0:r0 + rc, :], w_ref[k0:k0 + kk, :])
            acc = part if acc is None else acc + part
            k0 += kk
        h2 = h_ref[0, r0:r0 + rc, :] + mod[2:3] * acc
        h2_ref[0, r0:r0 + rc, :] = h2
        u2 = _rms_mod(h2, g_ref[...], mod[3:4], mod[4:5])
        u2_ref[0, r0:r0 + rc, :] = u2.astype(u2_ref.dtype)
        u_hi = u2.astype(BF16)
        u_lo = (u2 - u_hi.astype(F32)).astype(BF16)
        parts = _dot(u_hi, wr_hl) + _dot(u_lo, wr_hl)
        logits = parts + pltpu.roll(parts, LANE - N_EXPERTS, 1)
        lane = lax.broadcasted_iota(jnp.int32, logits.shape, 1)
        logits = jnp.where(lane < N_EXPERTS, logits, -jnp.inf)
        e = jnp.exp(logits - jnp.max(logits, axis=-1, keepdims=True))
        aff = e / jnp.sum(e, axis=-1, keepdims=True)
        aff_ref[0, r0:r0 + rc, :] = aff
        afft_ref[0, :, r0:r0 + rc] = aff.T[0:N_EXPERTS, :]


def _proj_out(ys, w, h, mod, layer, mod_row, g, wr, *, name):
    b, l, d = h.shape
    t = min(1024, l)
    mrow = (lambda bb: bb) if mod_row is None else (lambda bb: mod_row)
    in_specs = [pl.BlockSpec((1, t, y.shape[2]), lambda bb, i: (bb, i, 0)) for y in ys] + [
        pl.BlockSpec(w.shape, lambda bb, i: (0, 0)),
        pl.BlockSpec((1, t, d), lambda bb, i: (bb, i, 0)),
        pl.BlockSpec((1, 1, 6, d), lambda bb, i: (layer, mrow(bb), 0, 0)),
        pl.BlockSpec((1, d), lambda bb, i: (0, 0)),
        pl.BlockSpec((d, LANE), lambda bb, i: (0, 0)),
    ]
    return pl.pallas_call(
        functools.partial(_proj_out_kernel, n_y=len(ys)),
        grid=(b, l // t),
        in_specs=in_specs,
        out_specs=[
            pl.BlockSpec((1, t, d), lambda bb, i: (bb, i, 0)),
            pl.BlockSpec((1, t, d), lambda bb, i: (bb, i, 0)),
            pl.BlockSpec((1, t, LANE), lambda bb, i: (bb, i, 0)),
            pl.BlockSpec((1, N_EXPERTS, t), lambda bb, i: (bb, 0, i)),
        ],
        out_shape=[
            jax.ShapeDtypeStruct((b, l, d), F32),
            jax.ShapeDtypeStruct((b, l, d), BF16),
            jax.ShapeDtypeStruct((b, l, LANE), F32),
            jax.ShapeDtypeStruct((b, N_EXPERTS, l), F32),
        ],
        compiler_params=_cp("arbitrary", "arbitrary"),
        name=name,
    )(*ys, w, h, mod, g, wr)


def _excl_cumsum(mask):
    r, l = mask.shape
    x = jnp.where(mask, 1.0, 0.0)
    i0 = lax.broadcasted_iota(jnp.int32, (LANE, LANE), 0)
    i1 = lax.broadcasted_iota(jnp.int32, (LANE, LANE), 1)
    tri = jnp.where(i0 < i1, 1.0, 0.0).astype(BF16)
    carry = jnp.zeros((r, 1), F32)
    outs = []
    for c in range(l // LANE):
        xc = x[:, c * LANE:(c + 1) * LANE]
        outs.append(_dot(xc.astype(BF16), tri) + carry)
        carry = carry + jnp.sum(xc, axis=-1, keepdims=True)
    return jnp.concatenate(outs, axis=1)


def _route_kernel(a_ref, pos_ref, starts_ref, *, cap, tc):
    aff = a_ref[...]
    e, l = aff.shape
    capf = jnp.float32(cap)
    count = lambda mask: jnp.sum(jnp.where(mask, 1.0, 0.0), axis=-1, keepdims=True)

    def body(i, thr):
        cand = thr | jnp.left_shift(jnp.int32(1), 30 - i)
        return jnp.where(count(aff >= pltpu.bitcast(cand, F32)) >= capf, cand, thr)

    thr = lax.fori_loop(0, 31, body, jnp.zeros((e, 1), jnp.int32))
    t_lo = pltpu.bitcast(thr, F32)
    t_hi = pltpu.bitcast(jnp.maximum(thr + 1, F32_MIN_NORMAL_BITS), F32)
    above = aff >= t_hi
    sel0 = jnp.where(above, 1.0, 0.0)
    mid0 = jnp.where((aff >= t_lo) & jnp.logical_not(above), 1.0, 0.0)
    lane = lax.broadcasted_iota(jnp.int32, aff.shape, 1).astype(F32)

    def fill(carry):
        sel, mid, need = carry
        v = jnp.where(mid > 0.0, aff, -1.0)
        best = jnp.max(v, axis=-1, keepdims=True)
        first = jnp.min(jnp.where(v == best, lane, jnp.float32(l)), axis=-1, keepdims=True)
        pick = (lane == first) & (need > 0.0)
        return (jnp.where(pick, 1.0, sel), jnp.where(pick, 0.0, mid), need - jnp.where(need > 0.0, 1.0, 0.0))

    sel, _, _ = lax.while_loop(lambda carry: jnp.max(carry[2]) > 0.0, fill, (sel0, mid0, capf - count(above)))
    chosen = sel > 0.0
    pos = _excl_cumsum(chosen)
    pos_ref[...] = jnp.where(chosen, pos.astype(jnp.int32), -1)
    chunk = lax.broadcasted_iota(jnp.int32, (e, LANE), 1)
    starts = jnp.zeros((e, LANE), F32)
    for c in range(l // tc):
        starts = starts + jnp.where(chunk > c, count(chosen[:, c * tc:(c + 1) * tc]), 0.0)
    starts_ref[...] = starts.astype(jnp.int32)


def _route(afft, cap, tc, *, name):
    b, e, l = afft.shape
    pos, starts = pl.pallas_call(
        functools.partial(_route_kernel, cap=cap, tc=tc),
        out_shape=[jax.ShapeDtypeStruct((b * e, l), jnp.int32),
                   jax.ShapeDtypeStruct((b * e, LANE), jnp.int32)],
        compiler_params=pltpu.CompilerParams(vmem_limit_bytes=VMEM_LIMIT),
        name=name,
    )(afft.reshape(b * e, l))
    return pos.reshape(b, e, l), starts[:, :l // tc + 1].reshape(-1)


def _windows(st_ref, b, c, n_e, n_chunks, cap, win):
    starts, fits = [], None
    for e in range(n_e):
        k = (b * n_e + e) * (n_chunks + 1) + c
        start = jnp.minimum(st_ref[k] & -ROW_ALIGN, cap - win)
        ok = st_ref[k + 1] - start <= win
        starts.append(pl.multiple_of(start, ROW_ALIGN))
        fits = ok if fits is None else fits & ok
    return starts, fits


def _gather_kernel(st_ref, u_ref, pos_ref, o_ref, *, cap, tc, win):
    b = pl.program_id(0)
    n_e = pos_ref.shape[1]
    n_chunks = u_ref.shape[1] // tc
    o_ref[...] = jnp.zeros(o_ref.shape, o_ref.dtype)
    for c in range(n_chunks):
        x = u_ref[0, c * tc:(c + 1) * tc, :]
        pos = pos_ref[0, :, c * tc:(c + 1) * tc]
        starts, fits = _windows(st_ref, b, c, n_e, n_chunks, cap, win)

        @pl.when(fits)
        def _():
            slot = lax.broadcasted_iota(jnp.int32, (win, tc), 0)
            onehot = jnp.concatenate(
                [jnp.where(pos[e:e + 1] == slot + starts[e], 1.0, 0.0).astype(BF16) for e in range(n_e)], axis=0)
            y = _dot(onehot, x).astype(o_ref.dtype)
            for e in range(n_e):
                rows = pl.ds(starts[e], win)
                o_ref[0, e, rows, :] = o_ref[0, e, rows, :] + y[e * win:(e + 1) * win]

        @pl.when(jnp.logical_not(fits))
        def _():
            slot = lax.broadcasted_iota(jnp.int32, (cap, tc), 0)
            for e in range(n_e):
                onehot = jnp.where(pos[e:e + 1] == slot, 1.0, 0.0).astype(BF16)
                o_ref[0, e] = o_ref[0, e] + _dot(onehot, x).astype(o_ref.dtype)


def _gather(u2, pos, starts, cap, tc, win, *, name):
    b, l, d = u2.shape
    e = pos.shape[1]
    return pl.pallas_call(
        functools.partial(_gather_kernel, cap=cap, tc=tc, win=win),
        grid_spec=pltpu.PrefetchScalarGridSpec(
            num_scalar_prefetch=1,
            grid=(b,),
            in_specs=[
                pl.BlockSpec((1, l, d), lambda bb, st: (bb, 0, 0)),
                pl.BlockSpec((1, e, l), lambda bb, st: (bb, 0, 0)),
            ],
            out_specs=pl.BlockSpec((1, e, cap, d), lambda bb, st: (bb, 0, 0, 0)),
        ),
        out_shape=jax.ShapeDtypeStruct((b, e, cap, d), BF16),
        compiler_params=_cp("arbitrary"),
        name=name,
    )(starts, u2, pos)


def _ffn_kernel(*refs, n_x, fc):
    x_refs = refs[:n_x]
    wg_ref, wu_ref, wd_ref = refs[n_x:n_x + 3]
    y_refs = refs[n_x + 3:2 * n_x + 3]
    wg_s, wu_s, wd_s = refs[2 * n_x + 3:]

    @pl.when(pl.program_id(1) == 0)
    def _():
        wg_s[...] = wg_ref[0, 0].astype(BF16)
        wu_s[...] = wu_ref[0, 0].astype(BF16)
        wd_s[...] = wd_ref[0, 0].astype(BF16)

    d = x_refs[0].shape[3]
    rows = [r.shape[0] * r.shape[2] for r in x_refs]
    x = jnp.concatenate([r[:, 0].reshape(n, d) for r, n in zip(x_refs, rows)], axis=0)
    y = None
    for f0 in range(0, wg_s.shape[1], fc):
        a = _dot(x, wg_s[:, f0:f0 + fc])
        up = _dot(x, wu_s[:, f0:f0 + fc])
        hm = (a * jax.nn.sigmoid(a) * up).astype(BF16)
        part = _dot(hm, wd_s[f0:f0 + fc, :])
        y = part if y is None else y + part
    r0 = 0
    for y_ref, n in zip(y_refs, rows):
        y_ref[:, 0] = y[r0:r0 + n].reshape(y_ref.shape[0], y_ref.shape[2], d).astype(y_ref.dtype)
        r0 += n


def _ffn(xs, w_gate, w_up, w_down, layer, *, name):
    b, e, _, d = xs[0].shape
    f = w_gate.shape[3]
    bb = 4
    xspec = lambda x: pl.BlockSpec((bb, 1, x.shape[2], d), lambda ee, j: (j, ee, 0, 0))
    return pl.pallas_call(
        functools.partial(_ffn_kernel, n_x=len(xs), fc=512),
        grid=(e, b // bb),
        in_specs=[xspec(x) for x in xs] + [
            pl.BlockSpec((1, 1, d, f), lambda ee, j: (layer, ee, 0, 0)),
            pl.BlockSpec((1, 1, d, f), lambda ee, j: (layer, ee, 0, 0)),
            pl.BlockSpec((1, 1, f, d), lambda ee, j: (layer, ee, 0, 0)),
        ],
        out_specs=[xspec(x) for x in xs],
        out_shape=[jax.ShapeDtypeStruct(x.shape, BF16) for x in xs],
        scratch_shapes=[pltpu.VMEM((d, f), BF16), pltpu.VMEM((d, f), BF16), pltpu.VMEM((f, d), BF16)],
        compiler_params=_cp("arbitrary", "arbitrary", vmem=56 * 1024 * 1024),
        name=name,
    )(*xs, w_gate, w_up, w_down)


def _combine_kernel(st_ref, ye_ref, post_ref, ptf_ref, rep_ref, aff_ref, h_ref, mod_ref, o_ref,
                    *, cap, tc, win, n_chunks):
    b = pl.program_id(0)
    n_e = ye_ref.shape[1]
    d = ye_ref.shape[3]
    t = h_ref.shape[1]
    gate = mod_ref[0, 0][5:6]
    for cc in range(t // tc):
        c = pl.program_id(1) * (t // tc) + cc
        rows = slice(cc * tc, (cc + 1) * tc)
        pt = post_ref[0, rows, :]
        at = aff_ref[0, rows, :]
        starts, fits = _windows(st_ref, b, c, n_e, n_chunks, cap, win)

        def finish(w, y):
            o_ref[0, rows, :] = h_ref[0, rows, :] + gate * _dot(w, y)

        @pl.when(fits)
        def _():
            y = jnp.concatenate([ye_ref[0, e, pl.ds(starts[e], win), :] for e in range(n_e)], axis=0)
            rep = rep_ref[...]
            slot_of = _dot(ptf_ref[0, rows, :], rep)
            gate_of = _dot(at.astype(BF16), rep)
            lane = lax.broadcasted_iota(jnp.int32, (1, n_e * win), 1)
            want = lane % win
            for e in range(n_e):
                want = jnp.where(lane // win == e, want + starts[e], want)
            w = jnp.where(slot_of == want.astype(F32), gate_of, 0.0).astype(BF16)
            finish(w, y)

        @pl.when(jnp.logical_not(fits))
        def _():
            y = ye_ref[0].reshape(n_e * cap, d)
            if cap % LANE == 0:
                col = lax.broadcasted_iota(jnp.int32, (tc, cap), 1)
                w = jnp.concatenate(
                    [jnp.where(pt[:, e:e + 1] == col, at[:, e:e + 1], 0.0).astype(BF16) for e in range(n_e)], axis=1)
            else:
                col = lax.broadcasted_iota(jnp.int32, (tc, n_e * cap), 1)
                acc = jnp.zeros((tc, n_e * cap), F32)
                for e in range(n_e):
                    slot = pt[:, e:e + 1]
                    key = jnp.where(slot >= 0, slot + e * cap, -1)
                    acc = jnp.where(key == col, at[:, e:e + 1], acc)
                w = acc.astype(BF16)
            finish(w, y)


def _combine(ye, pos_t, aff, starts, h2, mod, layer, mod_row, tc, win, *, name):
    b, l, d = h2.shape
    e, cap = ye.shape[1], ye.shape[2]
    t = min(1024, l)
    mrow = (lambda bb: bb) if mod_row is None else (lambda bb: mod_row)
    assert cap <= 256 and LANE % win == 0
    pos_tf = jnp.pad(pos_t.astype(BF16), ((0, 0), (0, 0), (0, LANE - e)))
    rep = (jnp.arange(LANE)[:, None] == jnp.arange(e * win)[None, :] // win).astype(BF16)
    return pl.pallas_call(
        functools.partial(_combine_kernel, cap=cap, tc=tc, win=win, n_chunks=l // tc),
        grid_spec=pltpu.PrefetchScalarGridSpec(
            num_scalar_prefetch=1,
            grid=(b, l // t),
            in_specs=[
                pl.BlockSpec((1, e, cap, d), lambda bb, i, st: (bb, 0, 0, 0)),
                pl.BlockSpec((1, t, e), lambda bb, i, st: (bb, i, 0)),
                pl.BlockSpec((1, t, LANE), lambda bb, i, st: (bb, i, 0)),
                pl.BlockSpec((LANE, e * win), lambda bb, i, st: (0, 0)),
                pl.BlockSpec((1, t, LANE), lambda bb, i, st: (bb, i, 0)),
                pl.BlockSpec((1, t, d), lambda bb, i, st: (bb, i, 0)),
                pl.BlockSpec((1, 1, 6, d), lambda bb, i, st: (layer, mrow(bb), 0, 0)),
            ],
            out_specs=pl.BlockSpec((1, t, d), lambda bb, i, st: (bb, i, 0)),
        ),
        out_shape=jax.ShapeDtypeStruct((b, l, d), F32),
        compiler_params=_cp("arbitrary", "arbitrary"),
        name=name,
    )(starts, ye, pos_t, pos_tf, rep, aff, h2, mod)


_PERM64 = tuple(list(range(0, 16)) + list(range(32, 48)) + list(range(16, 32)) + list(range(48, 64)))


def _rope_tables(seq_len):
    t = jnp.arange(seq_len)
    row = (t // GRID_W).astype(F32)
    col = (t % GRID_W).astype(F32)
    n_freq = C_HEAD_DIM // 4
    inv = ROPE_THETA ** (-jnp.arange(n_freq, dtype=F32) / n_freq)
    ar, ac = row[:, None] * inv[None], col[:, None] * inv[None]
    c64 = jnp.concatenate([jnp.cos(ar), jnp.cos(ac), jnp.cos(ar), jnp.cos(ac)], axis=-1)
    s64 = jnp.concatenate([-jnp.sin(ar), -jnp.sin(ac), jnp.sin(ar), jnp.sin(ac)], axis=-1)
    reps = C_QK // C_HEAD_DIM
    return jnp.tile(c64, (1, reps)), jnp.tile(s64, (1, reps))


def kernel(x, c, ctx, c_ctx, w_ada, b_ada, g_mix, g_ffn, w_in_ab, conv_a, conv_b, conv_b_bias, ln_b_g, ln_b_b, w_out_ab, w_in_cd, g_q, g_k, lam_q1, lam_k1, lam_q2, lam_k2, g_subln, conv_d, hf_w1, hf_b1, hf_freq, hf_w2, hf_b2, hf_w3, hf_bias, w_out_cd, w_router, w_gate, w_up, w_down):
    bsz, seq_lat, d = x.shape
    seq_ctx = ctx.shape[1]
    cap_lat = max(1, (EC_CAPACITY * seq_lat) // N_EXPERTS)
    cap_ctx = max(1, (EC_CAPACITY * seq_ctx) // N_EXPERTS)

    cond = jnp.zeros((MOD_ROWS, d), F32).at[:bsz].set(c).at[CTX_ROW].set(c_ctx)
    mod = _ada(cond, w_ada, b_ada)

    perm = jnp.asarray(_PERM64)

    def permute_qk(w):
        quarter = C_HEAD_DIM // 4
        qk = w[:, :2 * C_QK].reshape(d, 2 * C_QK // C_HEAD_DIM, 2, 2, quarter)
        return jnp.concatenate([jnp.swapaxes(qk, 2, 3).reshape(d, 2 * C_QK), w[:, 2 * C_QK:]], axis=1)

    gi = jnp.arange(C_QK) // C_HEAD_DIM
    gsum = (gi[:, None] == gi[None, :]).astype(BF16)
    cos_t, sin_t = _rope_tables(seq_lat)
    dft_lat, dft_ctx = _dft_mats(seq_lat), _dft_mats(seq_ctx)

    h_lat, h_ctx = x, ctx
    for l in range(DEPTH):
        last = l == DEPTH - 1
        odd = l % 2 == 1
        i = l // 2
        run_ctx = odd or not last
        gm = g_mix[l][None]
        if not odd:
            w_in = w_in_ab[i].astype(BF16)
            w_out = w_out_ab[i].astype(BF16)
            cargs = (conv_a[i], conv_b[i], conv_b_bias[i], ln_b_g[i], ln_b_b[i])
            p_lat = _proj_in(h_lat, mod, l, None, gm, w_in, name=f"proj_in_lat{l}")
            ys_lat = [_conv_mix(p_lat, *cargs, name=f"conv_mix_lat{l}")]
            if not last:
                p_ctx = _proj_in(h_ctx, mod, l, CTX_ROW, gm, w_in, name=f"proj_in_ctx{l}")
                ys_ctx = [_conv_mix(p_ctx, *cargs, name=f"conv_mix_ctx{l}")]
        else:
            lam_init = 0.8 - 0.6 * math.exp(-0.3 * l)
            w_in = permute_qk(w_in_cd[i]).astype(BF16)
            w_out = w_out_cd[i].astype(BF16)
            gqk = jnp.stack([jnp.tile(g_q[i][perm], C_QK // C_HEAD_DIM) * (C_HEAD_DIM ** -0.5 * math.log2(math.e)),
                             jnp.tile(g_k[i][perm], C_QK // C_HEAD_DIM)])
            lamv = jnp.stack([lam_q1[i], lam_k1[i], lam_q2[i], lam_k2[i]])
            gsub = g_subln[i][None]
            p_lat = _proj_in(h_lat, mod, l, None, gm, w_in, norm_rows=(0, 1), gqk=gqk, gsum=gsum,
                             cos=cos_t, sin=sin_t, name=f"proj_in_lat{l}")
            if last:
                p_ctx = _proj_in(h_ctx, mod, l, CTX_ROW, gm, w_in[:, C_QK:2 * C_QK + C_V], norm_rows=(1,),
                                 gqk=gqk, gsum=gsum, name=f"proj_in_ctx{l}")
                kv_off = 0
            else:
                p_ctx = _proj_in(h_ctx, mod, l, CTX_ROW, gm, w_in, norm_rows=(0, 1), gqk=gqk, gsum=gsum,
                                 name=f"proj_in_ctx{l}")
                kv_off = C_HEADS
            o_lat = _attn(p_lat, p_ctx, kv_off, p_lat, lamv, gsub, lam_init, name=f"attn_lat{l}")
            fargs = (hf_w1[i], hf_b1[i], hf_freq[i], hf_w2[i], hf_b2[i], hf_w3[i])
            filt_lat = _hy_filter(seq_lat, *fargs, name=f"hy_filter_lat{l}")
            ys_lat = [o_lat, _hyena(p_lat, conv_d[i], filt_lat, hf_bias[i], dft_lat, f"lat{l}")]
            if not last:
                o_ctx = _attn(p_ctx, p_ctx, kv_off, None, lamv, gsub, lam_init, name=f"attn_ctx{l}")
                filt_ctx = _hy_filter(seq_ctx, *fargs, name=f"hy_filter_ctx{l}")
                ys_ctx = [o_ctx, _hyena(p_ctx, conv_d[i], filt_ctx, hf_bias[i], dft_ctx, f"ctx{l}")]

        gf = g_ffn[l][None]
        wr = jnp.pad(w_router[l], ((0, 0), (0, LANE - N_EXPERTS)))
        streams = [("lat", h_lat, ys_lat, None, cap_lat)]
        if not last:
            streams.append(("ctx", h_ctx, ys_ctx, CTX_ROW, cap_ctx))
        staged = []
        for tag, h, ys, mrow, cap in streams:
            h2, u2, aff, afft = _proj_out(ys, w_out, h, mod, l, mrow, gf, wr, name=f"proj_out_{tag}{l}")
            tc = min(MOE_CHUNK, h.shape[1])
            win = min(MOE_WINDOW, cap)
            pos, starts = _route(afft, cap, tc, name=f"route_{tag}{l}")
            xe = _gather(u2, pos, starts, cap, tc, win, name=f"gather_{tag}{l}")
            staged.append((tag, h2, aff, pos, starts, xe, mrow, tc, win))
        yes = _ffn([s[5] for s in staged], w_gate, w_up, w_down, l, name=f"ffn{l}")
        outs = []
        for (tag, h2, aff, pos, starts, _, mrow, tc, win), ye in zip(staged, yes):
            pos_t = jnp.transpose(pos, (0, 2, 1))
            outs.append(_combine(ye, pos_t, aff, starts, h2, mod, l, mrow, tc, win, name=f"combine_{tag}{l}"))
        h_lat = outs[0]
        if not last:
            h_ctx = outs[1]
    return h_lat
```

```python
import functools
import math

import jax
import jax.numpy as jnp
from jax import lax
from jax.experimental import pallas as pl
from jax.experimental.pallas import tpu as pltpu

F32 = jnp.float32
BF16 = jnp.bfloat16

D_MODEL = 1024
DEPTH = 4
GRID_W = 64
A_WIDTH = 512
B_WIDTH = 512
SHORT_CONV_W = 3
CONFORMER_CONV_W = 31
AB_IN = 3 * A_WIDTH + 2 * B_WIDTH
C_HEAD_DIM = 64
C_V_DIM = 128
C_HEADS = 4
C_QK = 512
C_V = 512
HY_WIDTH = 512
CD_IN = 2 * C_QK + C_V + 3 * HY_WIDTH
ROPE_THETA = 10000.0
HY_BANDS = 16
HY_EMB = 2 * HY_BANDS + 1
HY_MIN_DECAY = math.log(1e-2) / 1.5
HY_MAX_DECAY = math.log(1e-2) / 0.3
N_EXPERTS = 16
EXPERT_FF = 1024
EC_CAPACITY = 2
NORM_EPS = 1e-6
SUBLN_EPS = 1e-5

LANE = 128
SUBLANE = 8
HALO = 16
CONV_ROWS = 128
ROW_ALIGN = 16
MOE_CHUNK = 256
MOE_WINDOW = 64
ATTN_ROWS = 128
ROW_CHUNK = 256
MOD_ROWS = 16
CTX_ROW = 8
VMEM_LIMIT = 52 * 1024 * 1024
F32_MIN_NORMAL_BITS = 0x00800000


def _cp(*sem, vmem=VMEM_LIMIT):
    return pltpu.CompilerParams(dimension_semantics=sem, vmem_limit_bytes=vmem)


def _dot(a, b):
    return jnp.dot(a, b, preferred_element_type=F32)


def _dot3(a, b):
    a_hi = a.astype(BF16)
    a_lo = (a - a_hi.astype(F32)).astype(BF16)
    b_hi = b.astype(BF16)
    b_lo = (b - b_hi.astype(F32)).astype(BF16)
    return _dot(a_hi, b_hi) + (_dot(a_hi, b_lo) + _dot(a_lo, b_hi))


def _rms_mod(h, g, shift, scale):
    ms = jnp.mean(h * h, axis=-1, keepdims=True)
    return (h * lax.rsqrt(ms + NORM_EPS) * g) * (1.0 + scale) + shift


def _ada_kernel(c_ref, w_ref, b_ref, o_ref):
    c = c_ref[...]
    s = (c * jax.nn.sigmoid(c)).astype(BF16)
    o_ref[0] = _dot(s, w_ref[0].astype(BF16)) + b_ref[0]


def _ada(cond, w_ada, b_ada):
    depth, d, n = w_ada.shape
    tn = 1536
    out = pl.pallas_call(
        _ada_kernel,
        grid=(depth, n // tn),
        in_specs=[
            pl.BlockSpec((MOD_ROWS, d), lambda l, j: (0, 0)),
            pl.BlockSpec((1, d, tn), lambda l, j: (l, 0, j)),
            pl.BlockSpec((1, 1, tn), lambda l, j: (l, 0, j)),
        ],
        out_specs=pl.BlockSpec((1, MOD_ROWS, tn), lambda l, j: (l, 0, j)),
        out_shape=jax.ShapeDtypeStruct((depth, MOD_ROWS, n), F32),
        compiler_params=_cp("arbitrary", "arbitrary"),
        name="ada",
    )(cond, w_ada, b_ada.reshape(depth, 1, n))
    return out.reshape(depth, MOD_ROWS, 6, d)


def _swap32(x):
    n = x.shape[-1]
    lane = lax.broadcasted_iota(jnp.int32, x.shape, x.ndim - 1)
    up = pltpu.roll(x, n - 32, x.ndim - 1)
    dn = pltpu.roll(x, 32, x.ndim - 1)
    return jnp.where((lane & 32) == 0, up, dn)


def _proj_in_kernel(*refs, n_out, chunk, norm_rows, rope):
    h_ref, mod_ref, g_ref, w_ref = refs[:4]
    k = 4
    if norm_rows:
        gqk_ref, gsum_ref = refs[k:k + 2]
        k += 2
    if rope:
        cos_ref, sin_ref = refs[k:k + 2]
        k += 2
    o_ref, u_scr = refs[k], refs[k + 1]
    mod = mod_ref[0, 0]
    u = _rms_mod(h_ref[0], g_ref[...], mod[0:1], mod[1:2])
    u_scr[...] = u.astype(BF16)
    for n in range(n_out // chunk):
        p = _dot(u_scr[...], w_ref[:, n * chunk:(n + 1) * chunk])
        if n < len(norm_rows):
            ss = _dot((p * p).astype(BF16), gsum_ref[...])
            r = norm_rows[n]
            p = p * lax.rsqrt(ss * (1.0 / C_HEAD_DIM) + NORM_EPS) * gqk_ref[r:r + 1, :]
            if rope:
                p = p * cos_ref[...] + _swap32(p) * sin_ref[...]
        o_ref[0, :, n * chunk:(n + 1) * chunk] = p.astype(o_ref.dtype)


def _proj_in(h, mod, layer, mod_row, g, w, *, norm_rows=(), gqk=None, gsum=None, cos=None, sin=None, name):
    b, l, d = h.shape
    n_out = w.shape[1]
    t = min(1024, l)
    chunk = 512
    rope = cos is not None
    mrow = (lambda bb: bb) if mod_row is None else (lambda bb: mod_row)
    in_specs = [
        pl.BlockSpec((1, t, d), lambda i, bb: (bb, i, 0)),
        pl.BlockSpec((1, 1, 6, d), lambda i, bb: (layer, mrow(bb), 0, 0)),
        pl.BlockSpec((1, d), lambda i, bb: (0, 0)),
        pl.BlockSpec((d, n_out), lambda i, bb: (0, 0)),
    ]
    args = [h, mod, g, w]
    if norm_rows:
        in_specs += [pl.BlockSpec(gqk.shape, lambda i, bb: (0, 0)),
                     pl.BlockSpec(gsum.shape, lambda i, bb: (0, 0))]
        args += [gqk, gsum]
    if rope:
        in_specs += [pl.BlockSpec((t, chunk), lambda i, bb: (i, 0))] * 2
        args += [cos, sin]
    return pl.pallas_call(
        functools.partial(_proj_in_kernel, n_out=n_out, chunk=chunk, norm_rows=tuple(norm_rows), rope=rope),
        grid=(l // t, b),
        in_specs=in_specs,
        out_specs=pl.BlockSpec((1, t, n_out), lambda i, bb: (bb, i, 0)),
        out_shape=jax.ShapeDtypeStruct((b, l, n_out), BF16),
        scratch_shapes=[pltpu.VMEM((t, d), BF16)],
        compiler_params=_cp("arbitrary", "arbitrary"),
        name=name,
    )(*args)


def _halo_specs(t, l, width, col):
    r = t // HALO
    last = l // HALO - 1
    return [
        pl.BlockSpec((1, t, width), lambda bb, i: (bb, i, col)),
        pl.BlockSpec((1, HALO, width), lambda bb, i: (bb, jnp.maximum(i * r - 1, 0), col)),
        pl.BlockSpec((1, HALO, width), lambda bb, i: (bb, jnp.minimum((i + 1) * r, last), col)),
    ]


def _conv_mix_kernel(pc_ref, pp_ref, pn_ref, ca_ref, cb_ref, bb_ref, lg_ref, lb_ref, o_ref,
                     abuf, gbuf, zbuf, sbuf, wbuf, *, t, nt):
    i = pl.program_id(1)
    keep_prev = jnp.where(i == 0, 0.0, 1.0)
    keep_next = jnp.where(i == nt - 1, 0.0, 1.0)

    def prods(p):
        p = p.astype(F32)
        a = p[:, A_WIDTH:2 * A_WIDTH] * p[:, 2 * A_WIDTH:3 * A_WIDTH]
        g = p[:, 3 * A_WIDTH:3 * A_WIDTH + B_WIDTH] * jax.nn.sigmoid(p[:, 3 * A_WIDTH + B_WIDTH:])
        return a, g

    n_slab = B_WIDTH // LANE

    def put_g(r0, n, g):
        for c in range(n_slab):
            gbuf[c, r0:r0 + n, :] = g[:, c * LANE:(c + 1) * LANE]

    a, g = prods(pp_ref[0])
    abuf[0:HALO] = a * keep_prev
    put_g(0, HALO, g * keep_prev)
    a, g = prods(pc_ref[0])
    abuf[HALO:HALO + t] = a
    put_g(HALO, t, g)
    a, g = prods(pn_ref[0])
    abuf[HALO + t:2 * HALO + t] = a * keep_next
    put_g(HALO + t, HALO, g * keep_next)

    ca = ca_ref[...]
    ya = (ca[0:1] * abuf[HALO - 1:HALO - 1 + t] + ca[1:2] * abuf[HALO:HALO + t]
          + ca[2:3] * abuf[HALO + 1:HALO + 1 + t])
    o_ref[0, :, 0:A_WIDTH] = (pc_ref[0, :, 0:A_WIDTH].astype(F32) * ya).astype(o_ref.dtype)

    pad = (CONFORMER_CONV_W - 1) // 2
    rb = CONV_ROWS
    nv = rb // SUBLANE
    tail = gbuf.shape[1] - 2 * HALO - t
    put_g(2 * HALO + t, tail, jnp.zeros((tail, B_WIDTH), F32))

    def shift_rows(i, carry):
        base = pl.multiple_of(i * rb, rb)
        for c in range(n_slab):
            x = gbuf[c, pl.ds(base, rb + SUBLANE), :]
            for r in range(1, SUBLANE):
                sbuf[r - 1, c, pl.ds(base, rb), :] = x[r:r + rb]
        return carry

    lax.fori_loop(0, pl.cdiv(t + 2 * HALO - SUBLANE, rb), shift_rows, 0)
    for k in range(CONFORMER_CONV_W):
        wbuf[k] = jnp.broadcast_to(cb_ref[k:k + 1, :], (SUBLANE, B_WIDTH))

    def conv_rows(i, carry):
        base = pl.multiple_of(i * rb, rb)
        for c in range(n_slab):
            c0 = c * LANE
            acc = jnp.zeros((nv, SUBLANE, LANE), F32) + bb_ref[:, c0:c0 + LANE]
            for k in range(CONFORMER_CONV_W):
                s = HALO - pad + k
                r = s % SUBLANE
                src = gbuf.at[c] if r == 0 else sbuf.at[r - 1, c]
                rows = src[pl.ds(base + (s - r), rb), :].reshape(nv, SUBLANE, LANE)
                acc = acc + wbuf[k, :, c0:c0 + LANE] * rows
            zbuf[pl.ds(base, rb), c0:c0 + LANE] = acc.reshape(rb, LANE)
        return carry

    lax.fori_loop(0, t // rb, conv_rows, 0)
    z = zbuf[...]
    mu = jnp.mean(z, axis=-1, keepdims=True)
    zc = z - mu
    var = jnp.mean(zc * zc, axis=-1, keepdims=True)
    y = zc * lax.rsqrt(var + NORM_EPS) * lg_ref[...] + lb_ref[...]
    o_ref[0, :, A_WIDTH:] = (y * jax.nn.sigmoid(y)).astype(o_ref.dtype)


def _conv_mix(p, conv_a, conv_b, conv_b_bias, ln_g, ln_b, *, name):
    b, l, n = p.shape
    t = min(512, l)
    nt = l // t
    full = lambda shape: pl.BlockSpec(shape, lambda bb, i: (0,) * len(shape))
    return pl.pallas_call(
        functools.partial(_conv_mix_kernel, t=t, nt=nt),
        grid=(b, nt),
        in_specs=_halo_specs(t, l, n, 0) + [
            full(conv_a.shape), full(conv_b.shape), full((1, B_WIDTH)), full((1, B_WIDTH)), full((1, B_WIDTH))],
        out_specs=pl.BlockSpec((1, t, A_WIDTH + B_WIDTH), lambda bb, i: (bb, i, 0)),
        out_shape=jax.ShapeDtypeStruct((b, l, A_WIDTH + B_WIDTH), BF16),
        scratch_shapes=[pltpu.VMEM((t + 2 * HALO, A_WIDTH), F32),
                        pltpu.VMEM((B_WIDTH // LANE, t + 2 * HALO + CONV_ROWS, LANE), F32),
                        pltpu.VMEM((t, B_WIDTH), F32),
                        pltpu.VMEM((SUBLANE - 1, B_WIDTH // LANE, t + 2 * HALO + CONV_ROWS, LANE), F32),
                        pltpu.VMEM((CONFORMER_CONV_W, SUBLANE, B_WIDTH), F32)],
        compiler_params=_cp("arbitrary", "arbitrary"),
        name=name,
    )(p, p, p, conv_a, conv_b, conv_b_bias[None], ln_g[None], ln_b[None])


def _attn_kernel(*refs, lam_init, has_lat):
    if has_lat:
        q_ref, kc_ref, vc_ref, kl_ref, vl_ref, lam_ref, gs_ref, o_ref = refs
    else:
        q_ref, kc_ref, vc_ref, lam_ref, gs_ref, o_ref = refs
    lv = lam_ref[...]
    lam = (jnp.exp(jnp.sum(lv[0:1] * lv[1:2], axis=-1, keepdims=True))
           - jnp.exp(jnp.sum(lv[2:3] * lv[3:4], axis=-1, keepdims=True)) + lam_init)
    q = q_ref[0]
    lane = lax.broadcasted_iota(jnp.int32, q.shape, 1)
    nt = (((1,), (1,)), ((), ()))
    ones_col = lambda v: jnp.concatenate([v, jnp.ones_like(v)], axis=1)
    vc = ones_col(vc_ref[0])
    if has_lat:
        vl = ones_col(vl_ref[0])
    halves = []
    tq = q.shape[0]
    rc = min(ATTN_ROWS, tq)
    for j in range(2):
        qj = jnp.where((lane >= C_HEAD_DIM) == (j == 1), q, jnp.zeros_like(q))
        outs = []
        for r0 in range(0, tq, rc):
            qr = qj[r0:r0 + rc]
            s_c = lax.dot_general(qr, kc_ref[0], nt, preferred_element_type=F32)
            m = jnp.max(s_c, axis=-1, keepdims=True)
            if has_lat:
                s_l = lax.dot_general(qr, kl_ref[0], nt, preferred_element_type=F32)
                m = jnp.maximum(m, jnp.max(s_l, axis=-1, keepdims=True))
            oj = _dot(jnp.exp2(s_c - m).astype(BF16), vc)
            if has_lat:
                oj = oj + _dot(jnp.exp2(s_l - m).astype(BF16), vl)
            outs.append(oj[:, 0:C_V_DIM] * (1.0 / oj[:, C_V_DIM:C_V_DIM + 1]))
        halves.append(jnp.concatenate(outs, axis=0))
    o = halves[0] - lam * halves[1]
    ms = jnp.mean(o * o, axis=-1, keepdims=True)
    o = o * lax.rsqrt(ms + SUBLN_EPS) * gs_ref[...] * (1.0 - lam_init)
    o_ref[0] = o.astype(o_ref.dtype)


def _attn(pq, pkv_ctx, kv_off, p_lat, lamv, g_sub, lam_init, *, name):
    b, lq, _ = pq.shape
    lc = pkv_ctx.shape[1]
    tq = min(1024, lq)
    has_lat = p_lat is not None
    in_specs = [
        pl.BlockSpec((1, tq, LANE), lambda bb, h, i: (bb, i, h)),
        pl.BlockSpec((1, lc, LANE), lambda bb, h, i: (bb, 0, kv_off + h)),
        pl.BlockSpec((1, lc, LANE), lambda bb, h, i: (bb, 0, kv_off + C_HEADS + h)),
    ]
    args = [pq, pkv_ctx, pkv_ctx]
    if has_lat:
        ll = p_lat.shape[1]
        in_specs += [pl.BlockSpec((1, ll, LANE), lambda bb, h, i: (bb, 0, C_HEADS + h)),
                     pl.BlockSpec((1, ll, LANE), lambda bb, h, i: (bb, 0, 2 * C_HEADS + h))]
        args += [p_lat, p_lat]
    in_specs += [pl.BlockSpec((4, C_HEAD_DIM), lambda bb, h, i: (0, 0)),
                 pl.BlockSpec((1, C_V_DIM), lambda bb, h, i: (0, 0))]
    args += [lamv, g_sub]
    return pl.pallas_call(
        functools.partial(_attn_kernel, lam_init=lam_init, has_lat=has_lat),
        grid=(b, C_HEADS, lq // tq),
        in_specs=in_specs,
        out_specs=pl.BlockSpec((1, tq, LANE), lambda bb, h, i: (bb, i, h)),
        out_shape=jax.ShapeDtypeStruct((b, lq, C_V), BF16),
        compiler_params=_cp("arbitrary", "arbitrary", "arbitrary"),
        name=name,
    )(*args)


def _hy_pre_kernel(pc_ref, pp_ref, pn_ref, cd_ref, o_ref, buf, *, t, nt):
    i = pl.program_id(1)
    buf[0:HALO] = pp_ref[0].astype(F32) * jnp.where(i == 0, 0.0, 1.0)
    buf[HALO:HALO + t] = pc_ref[0].astype(F32)
    buf[HALO + t:2 * HALO + t] = pn_ref[0].astype(F32) * jnp.where(i == nt - 1, 0.0, 1.0)
    cd = cd_ref[...]
    z = (cd[0:1] * buf[HALO - 1:HALO - 1 + t] + cd[1:2] * buf[HALO:HALO + t]
         + cd[2:3] * buf[HALO + 1:HALO + 1 + t])
    o_ref[0, :, 0:HY_WIDTH] = z[:, 0:HY_WIDTH].astype(o_ref.dtype)
    o_ref[0, :, HY_WIDTH:] = (z[:, HY_WIDTH:2 * HY_WIDTH] * z[:, 2 * HY_WIDTH:]).astype(o_ref.dtype)


def _hy_pre(p, conv_d, *, name):
    b, l, _ = p.shape
    t = min(512, l)
    nt = l // t
    w = 3 * HY_WIDTH
    return pl.pallas_call(
        functools.partial(_hy_pre_kernel, t=t, nt=nt),
        grid=(b, nt),
        in_specs=_halo_specs(t, l, w, 1) + [pl.BlockSpec(conv_d.shape, lambda bb, i: (0, 0))],
        out_specs=pl.BlockSpec((1, t, 2 * HY_WIDTH), lambda bb, i: (bb, i, 0)),
        out_shape=jax.ShapeDtypeStruct((b, l, 2 * HY_WIDTH), BF16),
        scratch_shapes=[pltpu.VMEM((t + 2 * HALO, w), F32)],
        compiler_params=_cp("arbitrary", "arbitrary"),
        name=name,
    )(p, p, p, conv_d)


def _hy_filter_kernel(z_ref, w1_ref, b1_ref, fr_ref, w2_ref, b2_ref, w3_ref, dist_ref, dec_ref, o_ref):
    fr = fr_ref[...]
    hid = jnp.sin(fr * (_dot3(z_ref[...], w1_ref[...]) + b1_ref[...]))
    hid = jnp.sin(fr * (_dot3(hid, w2_ref[...]) + b2_ref[...]))
    h = _dot3(hid, w3_ref[...])
    o_ref[...] = h * jnp.exp(-dist_ref[...] * dec_ref[...])


def _hy_filter(seq_len, w1, b1, freq, w2, b2, w3, *, name):
    t = jnp.arange(seq_len, dtype=F32)
    t_unit = jnp.linspace(0.0, 1.0, seq_len)[:, None]
    bands = jnp.linspace(1e-4, HY_BANDS - 1, HY_BANDS)
    ang = (2.0 * math.pi / seq_len) * t[:, None] * bands[None]
    z = jnp.concatenate([t_unit, jnp.cos(ang), -jnp.sin(ang)], axis=-1)
    centre = seq_len // 2
    dist = (jnp.abs(t - centre) / max(centre, 1))[:, None]
    decay = jnp.abs(jnp.linspace(HY_MIN_DECAY, HY_MAX_DECAY, HY_WIDTH))[None]
    pad_c = lambda a, n: jnp.pad(a, ((0, 0), (0, n - a.shape[1])))
    pad_r = lambda a, n: jnp.pad(a, ((0, n - a.shape[0]), (0, 0)))
    args = [pad_c(z, LANE), pad_c(pad_r(w1, LANE), LANE), pad_c(b1[None], LANE), pad_c(freq[None], LANE),
            pad_c(pad_r(w2, LANE), LANE), pad_c(b2[None], LANE), pad_r(w3, LANE), dist, decay]
    return pl.pallas_call(
        _hy_filter_kernel,
        out_shape=jax.ShapeDtypeStruct((seq_len, HY_WIDTH), F32),
        compiler_params=pltpu.CompilerParams(vmem_limit_bytes=VMEM_LIMIT),
        name=name,
    )(*args)


def _dft_mats(l):
    m = 3 * l // 2
    hh = m // 2
    f = jnp.arange(hh, dtype=jnp.int32)
    s = jnp.arange(l, dtype=jnp.int32)
    n = s + l // 2

    def cos_sin(rows, n_cols):
        w = 64
        phase = lambda k: (2.0 * math.pi / m) * ((rows[:, None] * k[None, :]) % m).astype(F32)
        a = phase(jnp.arange(0, n_cols, w, dtype=jnp.int32))[:, :, None]
        b = phase(jnp.arange(w, dtype=jnp.int32))[:, None, :]
        ca, sa, cb, sb = jnp.cos(a), jnp.sin(a), jnp.cos(b), jnp.sin(b)
        shape = (rows.shape[0], n_cols)
        return (ca * cb - sa * sb).reshape(shape), (sa * cb + ca * sb).reshape(shape)

    c, sn = cos_sin(f, l)
    nyq = jnp.where(s % 2 == 0, 1.0, -1.0).astype(F32)
    fw = jnp.stack([c, jnp.where(f[:, None] == 0, nyq[None, :], -sn)]).astype(BF16)
    c, sn = cos_sin(n, hh)
    nyq_n = jnp.where(n % 2 == 0, 1.0, -1.0).astype(F32)
    iv_r = jnp.where(f[None, :] == 0, 1.0, 2.0 * c)
    iv_i = jnp.where(f[None, :] == 0, nyq_n[:, None], -2.0 * sn)
    iv = jnp.concatenate([iv_r, iv_i], axis=1).astype(BF16)
    return fw, iv


def _dft_fwd_kernel(*refs, tf):
    fw_ref, v_ref = refs[:2]
    o_ref = refs[-1]
    v = v_ref[0].astype(BF16)
    ur = _dot(fw_ref[0], v)
    ui = _dot(fw_ref[1], v)
    if len(refs) == 3:
        o_ref[0, 0] = ur.astype(o_ref.dtype)
        o_ref[0, 1] = ui.astype(o_ref.dtype)
        return
    h_ref = refs[2]
    hr = h_ref[0]
    hi = h_ref[1]
    row = lax.broadcasted_iota(jnp.int32, ur.shape, 0) + pl.program_id(0) * tf
    packed = row == 0
    uihi = ui * hi
    o_ref[0, 0] = (ur * hr - jnp.where(packed, 0.0, uihi)).astype(o_ref.dtype)
    o_ref[0, 1] = jnp.where(packed, uihi, ur * hi + ui * hr).astype(o_ref.dtype)


def _dft_fwd(fw, v, col, hspec, out_dtype, *, name):
    b, l, _ = v.shape
    hh = fw.shape[1]
    tf = min(512, hh)
    in_specs = [
        pl.BlockSpec((2, tf, l), lambda i, bb: (0, i, 0)),
        pl.BlockSpec((1, l, HY_WIDTH), lambda i, bb: (bb, 0, col)),
    ]
    args = [fw, v]
    if hspec is not None:
        in_specs.append(pl.BlockSpec((2, tf, HY_WIDTH), lambda i, bb: (0, i, 0)))
        args.append(hspec)
    return pl.pallas_call(
        functools.partial(_dft_fwd_kernel, tf=tf),
        grid=(hh // tf, b),
        in_specs=in_specs,
        out_specs=pl.BlockSpec((1, 2, tf, HY_WIDTH), lambda i, bb: (bb, 0, i, 0)),
        out_shape=jax.ShapeDtypeStruct((b, 2, hh, HY_WIDTH), out_dtype),
        compiler_params=_cp("arbitrary", "arbitrary"),
        name=name,
    )(*args)


def _dft_inv_kernel(iv_ref, y_ref, gv_ref, bias_ref, o_ref, *, inv_m):
    m = iv_ref.shape[1]
    yhat = y_ref[0].reshape(m, HY_WIDTH)
    conv = _dot(iv_ref[...], yhat) * inv_m
    gv = gv_ref[0].astype(F32)
    o_ref[0] = (gv[:, 0:HY_WIDTH] * (conv + gv[:, HY_WIDTH:] * bias_ref[...])).astype(o_ref.dtype)


def _dft_inv(iv, yhat, gv, bias, *, name):
    b, l, _ = gv.shape
    m = iv.shape[1]
    tn = min(1024, l)
    return pl.pallas_call(
        functools.partial(_dft_inv_kernel, inv_m=1.0 / m),
        grid=(l // tn, b),
        in_specs=[
            pl.BlockSpec((tn, m), lambda i, bb: (i, 0)),
            pl.BlockSpec((1, 2, m // 2, HY_WIDTH), lambda i, bb: (bb, 0, 0, 0)),
            pl.BlockSpec((1, tn, 2 * HY_WIDTH), lambda i, bb: (bb, i, 0)),
            pl.BlockSpec((1, HY_WIDTH), lambda i, bb: (0, 0)),
        ],
        out_specs=pl.BlockSpec((1, tn, HY_WIDTH), lambda i, bb: (bb, i, 0)),
        out_shape=jax.ShapeDtypeStruct((b, l, HY_WIDTH), BF16),
        compiler_params=_cp("arbitrary", "arbitrary"),
        name=name,
    )(iv, yhat, gv, bias)


def _hyena(p, conv_d, filt, bias, dft, tag):
    fw, iv = dft
    fhat = _dft_fwd(fw, filt[None], 0, None, F32, name="hy_fhat_" + tag)[0]
    gv = _hy_pre(p, conv_d, name="hy_pre_" + tag)
    yhat = _dft_fwd(fw, gv, 1, fhat, BF16, name="hy_fwd_" + tag)
    return _dft_inv(iv, yhat, gv, bias[None], name="hy_inv_" + tag)


def _proj_out_kernel(*refs, n_y):
    y_refs = refs[:n_y]
    w_ref, h_ref, mod_ref, g_ref, wr_ref, h2_ref, u2_ref, aff_ref, afft_ref = refs[n_y:]
    mod = mod_ref[0, 0]
    wr = wr_ref[...]
    wr_hi = wr.astype(BF16).astype(F32)
    wr_hl = (wr_hi + pltpu.roll(wr - wr_hi, N_EXPERTS, 1)).astype(BF16)
    t = h_ref.shape[1]
    rc = min(ROW_CHUNK, t)
    for r0 in range(0, t, rc):
        acc = None
        k0 = 0
        for y_ref in y_refs:
            kk = y_ref.shape[2]
            part = _dot(y_ref[0, r0:r0 + rc, :], w_ref[k0:k0 + kk, :])
            acc = part if acc is None else acc + part
            k0 += kk
        h2 = h_ref[0, r0:r0 + rc, :] + mod[2:3] * acc
        h2_ref[0, r0:r0 + rc, :] = h2
        u2 = _rms_mod(h2, g_ref[...], mod[3:4], mod[4:5])
        u2_ref[0, r0:r0 + rc, :] = u2.astype(u2_ref.dtype)
        u_hi = u2.astype(BF16)
        u_lo = (u2 - u_hi.astype(F32)).astype(BF16)
        parts = _dot(u_hi, wr_hl) + _dot(u_lo, wr_hl)
        logits = parts + pltpu.roll(parts, LANE - N_EXPERTS, 1)
        lane = lax.broadcasted_iota(jnp.int32, logits.shape, 1)
        logits = jnp.where(lane < N_EXPERTS, logits, -jnp.inf)
        e = jnp.exp(logits - jnp.max(logits, axis=-1, keepdims=True))
        aff = e / jnp.sum(e, axis=-1, keepdims=True)
        aff_ref[0, r0:r0 + rc, :] = aff
        afft_ref[0, :, r0:r0 + rc] = aff.T[0:N_EXPERTS, :]


def _proj_out(ys, w, h, mod, layer, mod_row, g, wr, *, name):
    b, l, d = h.shape
    t = min(1024, l)
    mrow = (lambda bb: bb) if mod_row is None else (lambda bb: mod_row)
    in_specs = [pl.BlockSpec((1, t, y.shape[2]), lambda bb, i: (bb, i, 0)) for y in ys] + [
        pl.BlockSpec(w.shape, lambda bb, i: (0, 0)),
        pl.BlockSpec((1, t, d), lambda bb, i: (bb, i, 0)),
        pl.BlockSpec((1, 1, 6, d), lambda bb, i: (layer, mrow(bb), 0, 0)),
        pl.BlockSpec((1, d), lambda bb, i: (0, 0)),
        pl.BlockSpec((d, LANE), lambda bb, i: (0, 0)),
    ]
    return pl.pallas_call(
        functools.partial(_proj_out_kernel, n_y=len(ys)),
        grid=(b, l // t),
        in_specs=in_specs,
        out_specs=[
            pl.BlockSpec((1, t, d), lambda bb, i: (bb, i, 0)),
            pl.BlockSpec((1, t, d), lambda bb, i: (bb, i, 0)),
            pl.BlockSpec((1, t, LANE), lambda bb, i: (bb, i, 0)),
            pl.BlockSpec((1, N_EXPERTS, t), lambda bb, i: (bb, 0, i)),
        ],
        out_shape=[
            jax.ShapeDtypeStruct((b, l, d), F32),
            jax.ShapeDtypeStruct((b, l, d), BF16),
            jax.ShapeDtypeStruct((b, l, LANE), F32),
            jax.ShapeDtypeStruct((b, N_EXPERTS, l), F32),
        ],
        compiler_params=_cp("arbitrary", "arbitrary"),
        name=name,
    )(*ys, w, h, mod, g, wr)


def _excl_cumsum(mask):
    r, l = mask.shape
    x = jnp.where(mask, 1.0, 0.0)
    i0 = lax.broadcasted_iota(jnp.int32, (LANE, LANE), 0)
    i1 = lax.broadcasted_iota(jnp.int32, (LANE, LANE), 1)
    tri = jnp.where(i0 < i1, 1.0, 0.0).astype(BF16)
    carry = jnp.zeros((r, 1), F32)
    outs = []
    for c in range(l // LANE):
        xc = x[:, c * LANE:(c + 1) * LANE]
        outs.append(_dot(xc.astype(BF16), tri) + carry)
        carry = carry + jnp.sum(xc, axis=-1, keepdims=True)
    return jnp.concatenate(outs, axis=1)


def _route_kernel(a_ref, pos_ref, starts_ref, *, cap, tc):
    aff = a_ref[...]
    e, l = aff.shape
    capf = jnp.float32(cap)
    count = lambda mask: jnp.sum(jnp.where(mask, 1.0, 0.0), axis=-1, keepdims=True)

    def body(i, thr):
        cand = thr | jnp.left_shift(jnp.int32(1), 30 - i)
        return jnp.where(count(aff >= pltpu.bitcast(cand, F32)) >= capf, cand, thr)

    thr = lax.fori_loop(0, 31, body, jnp.zeros((e, 1), jnp.int32))
    t_lo = pltpu.bitcast(thr, F32)
    t_hi = pltpu.bitcast(jnp.maximum(thr + 1, F32_MIN_NORMAL_BITS), F32)
    above = aff >= t_hi
    sel0 = jnp.where(above, 1.0, 0.0)
    mid0 = jnp.where((aff >= t_lo) & jnp.logical_not(above), 1.0, 0.0)
    lane = lax.broadcasted_iota(jnp.int32, aff.shape, 1).astype(F32)

    def fill(carry):
        sel, mid, need = carry
        v = jnp.where(mid > 0.0, aff, -1.0)
        best = jnp.max(v, axis=-1, keepdims=True)
        first = jnp.min(jnp.where(v == best, lane, jnp.float32(l)), axis=-1, keepdims=True)
        pick = (lane == first) & (need > 0.0)
        return (jnp.where(pick, 1.0, sel), jnp.where(pick, 0.0, mid), need - jnp.where(need > 0.0, 1.0, 0.0))

    sel, _, _ = lax.while_loop(lambda carry: jnp.max(carry[2]) > 0.0, fill, (sel0, mid0, capf - count(above)))
    chosen = sel > 0.0
    pos = _excl_cumsum(chosen)
    pos_ref[...] = jnp.where(chosen, pos.astype(jnp.int32), -1)
    chunk = lax.broadcasted_iota(jnp.int32, (e, LANE), 1)
    starts = jnp.zeros((e, LANE), F32)
    for c in range(l // tc):
        starts = starts + jnp.where(chunk > c, count(chosen[:, c * tc:(c + 1) * tc]), 0.0)
    starts_ref[...] = starts.astype(jnp.int32)


def _route(afft, cap, tc, *, name):
    b, e, l = afft.shape
    pos, starts = pl.pallas_call(
        functools.partial(_route_kernel, cap=cap, tc=tc),
        out_shape=[jax.ShapeDtypeStruct((b * e, l), jnp.int32),
                   jax.ShapeDtypeStruct((b * e, LANE), jnp.int32)],
        compiler_params=pltpu.CompilerParams(vmem_limit_bytes=VMEM_LIMIT),
        name=name,
    )(afft.reshape(b * e, l))
    return pos.reshape(b, e, l), starts[:, :l // tc + 1].reshape(-1)


def _windows(st_ref, b, c, n_e, n_chunks, cap, win):
    starts, fits = [], None
    for e in range(n_e):
        k = (b * n_e + e) * (n_chunks + 1) + c
        start = jnp.minimum(st_ref[k] & -ROW_ALIGN, cap - win)
        ok = st_ref[k + 1] - start <= win
        starts.append(pl.multiple_of(start, ROW_ALIGN))
        fits = ok if fits is None else fits & ok
    return starts, fits


def _gather_kernel(st_ref, u_ref, pos_ref, o_ref, *, cap, tc, win):
    b = pl.program_id(0)
    n_e = pos_ref.shape[1]
    n_chunks = u_ref.shape[1] // tc
    o_ref[...] = jnp.zeros(o_ref.shape, o_ref.dtype)
    for c in range(n_chunks):
        x = u_ref[0, c * tc:(c + 1) * tc, :]
        pos = pos_ref[0, :, c * tc:(c + 1) * tc]
        starts, fits = _windows(st_ref, b, c, n_e, n_chunks, cap, win)

        @pl.when(fits)
        def _():
            slot = lax.broadcasted_iota(jnp.int32, (win, tc), 0)
            onehot = jnp.concatenate(
                [jnp.where(pos[e:e + 1] == slot + starts[e], 1.0, 0.0).astype(BF16) for e in range(n_e)], axis=0)
            y = _dot(onehot, x).astype(o_ref.dtype)
            for e in range(n_e):
                rows = pl.ds(starts[e], win)
                o_ref[0, e, rows, :] = o_ref[0, e, rows, :] + y[e * win:(e + 1) * win]

        @pl.when(jnp.logical_not(fits))
        def _():
            slot = lax.broadcasted_iota(jnp.int32, (cap, tc), 0)
            for e in range(n_e):
                onehot = jnp.where(pos[e:e + 1] == slot, 1.0, 0.0).astype(BF16)
                o_ref[0, e] = o_ref[0, e] + _dot(onehot, x).astype(o_ref.dtype)


def _gather(u2, pos, starts, cap, tc, win, *, name):
    b, l, d = u2.shape
    e = pos.shape[1]
    return pl.pallas_call(
        functools.partial(_gather_kernel, cap=cap, tc=tc, win=win),
        grid_spec=pltpu.PrefetchScalarGridSpec(
            num_scalar_prefetch=1,
            grid=(b,),
            in_specs=[
                pl.BlockSpec((1, l, d), lambda bb, st: (bb, 0, 0)),
                pl.BlockSpec((1, e, l), lambda bb, st: (bb, 0, 0)),
            ],
            out_specs=pl.BlockSpec((1, e, cap, d), lambda bb, st: (bb, 0, 0, 0)),
        ),
        out_shape=jax.ShapeDtypeStruct((b, e, cap, d), BF16),
        compiler_params=_cp("arbitrary"),
        name=name,
    )(starts, u2, pos)


def _ffn_kernel(*refs, n_x, fc):
    x_refs = refs[:n_x]
    wg_ref, wu_ref, wd_ref = refs[n_x:n_x + 3]
    y_refs = refs[n_x + 3:2 * n_x + 3]
    wg_s, wu_s, wd_s = refs[2 * n_x + 3:]

    @pl.when(pl.program_id(1) == 0)
    def _():
        wg_s[...] = wg_ref[0, 0].astype(BF16)
        wu_s[...] = wu_ref[0, 0].astype(BF16)
        wd_s[...] = wd_ref[0, 0].astype(BF16)

    d = x_refs[0].shape[3]
    rows = [r.shape[0] * r.shape[2] for r in x_refs]
    x = jnp.concatenate([r[:, 0].reshape(n, d) for r, n in zip(x_refs, rows)], axis=0)
    y = None
    for f0 in range(0, wg_s.shape[1], fc):
        a = _dot(x, wg_s[:, f0:f0 + fc])
        up = _dot(x, wu_s[:, f0:f0 + fc])
        hm = (a * jax.nn.sigmoid(a) * up).astype(BF16)
        part = _dot(hm, wd_s[f0:f0 + fc, :])
        y = part if y is None else y + part
    r0 = 0
    for y_ref, n in zip(y_refs, rows):
        y_ref[:, 0] = y[r0:r0 + n].reshape(y_ref.shape[0], y_ref.shape[2], d).astype(y_ref.dtype)
        r0 += n


def _ffn(xs, w_gate, w_up, w_down, layer, *, name):
    b, e, _, d = xs[0].shape
    f = w_gate.shape[3]
    bb = 4
    xspec = lambda x: pl.BlockSpec((bb, 1, x.shape[2], d), lambda ee, j: (j, ee, 0, 0))
    return pl.pallas_call(
        functools.partial(_ffn_kernel, n_x=len(xs), fc=512),
        grid=(e, b // bb),
        in_specs=[xspec(x) for x in xs] + [
            pl.BlockSpec((1, 1, d, f), lambda ee, j: (layer, ee, 0, 0)),
            pl.BlockSpec((1, 1, d, f), lambda ee, j: (layer, ee, 0, 0)),
            pl.BlockSpec((1, 1, f, d), lambda ee, j: (layer, ee, 0, 0)),
        ],
        out_specs=[xspec(x) for x in xs],
        out_shape=[jax.ShapeDtypeStruct(x.shape, BF16) for x in xs],
        scratch_shapes=[pltpu.VMEM((d, f), BF16), pltpu.VMEM((d, f), BF16), pltpu.VMEM((f, d), BF16)],
        compiler_params=_cp("arbitrary", "arbitrary", vmem=56 * 1024 * 1024),
        name=name,
    )(*xs, w_gate, w_up, w_down)


def _combine_kernel(st_ref, ye_ref, post_ref, ptf_ref, rep_ref, aff_ref, h_ref, mod_ref, o_ref,
                    *, cap, tc, win, n_chunks):
    b = pl.program_id(0)
    n_e = ye_ref.shape[1]
    d = ye_ref.shape[3]
    t = h_ref.shape[1]
    gate = mod_ref[0, 0][5:6]
    for cc in range(t // tc):
        c = pl.program_id(1) * (t // tc) + cc
        rows = slice(cc * tc, (cc + 1) * tc)
        pt = post_ref[0, rows, :]
        at = aff_ref[0, rows, :]
        starts, fits = _windows(st_ref, b, c, n_e, n_chunks, cap, win)

        def finish(w, y):
            o_ref[0, rows, :] = h_ref[0, rows, :] + gate * _dot(w, y)

        @pl.when(fits)
        def _():
            y = jnp.concatenate([ye_ref[0, e, pl.ds(starts[e], win), :] for e in range(n_e)], axis=0)
            rep = rep_ref[...]
            slot_of = _dot(ptf_ref[0, rows, :], rep)
            gate_of = _dot(at.astype(BF16), rep)
            lane = lax.broadcasted_iota(jnp.int32, (1, n_e * win), 1)
            want = lane % win
            for e in range(n_e):
                want = jnp.where(lane // win == e, want + starts[e], want)
            w = jnp.where(slot_of == want.astype(F32), gate_of, 0.0).astype(BF16)
            finish(w, y)

        @pl.when(jnp.logical_not(fits))
        def _():
            y = ye_ref[0].reshape(n_e * cap, d)
            if cap % LANE == 0:
                col = lax.broadcasted_iota(jnp.int32, (tc, cap), 1)
                w = jnp.concatenate(
                    [jnp.where(pt[:, e:e + 1] == col, at[:, e:e + 1], 0.0).astype(BF16) for e in range(n_e)], axis=1)
            else:
                col = lax.broadcasted_iota(jnp.int32, (tc, n_e * cap), 1)
                acc = jnp.zeros((tc, n_e * cap), F32)
                for e in range(n_e):
                    slot = pt[:, e:e + 1]
                    key = jnp.where(slot >= 0, slot + e * cap, -1)
                    acc = jnp.where(key == col, at[:, e:e + 1], acc)
                w = acc.astype(BF16)
            finish(w, y)


def _combine(ye, pos_t, aff, starts, h2, mod, layer, mod_row, tc, win, *, name):
    b, l, d = h2.shape
    e, cap = ye.shape[1], ye.shape[2]
    t = min(1024, l)
    mrow = (lambda bb: bb) if mod_row is None else (lambda bb: mod_row)
    assert cap <= 256 and LANE % win == 0
    pos_tf = jnp.pad(pos_t.astype(BF16), ((0, 0), (0, 0), (0, LANE - e)))
    rep = (jnp.arange(LANE)[:, None] == jnp.arange(e * win)[None, :] // win).astype(BF16)
    return pl.pallas_call(
        functools.partial(_combine_kernel, cap=cap, tc=tc, win=win, n_chunks=l // tc),
        grid_spec=pltpu.PrefetchScalarGridSpec(
            num_scalar_prefetch=1,
            grid=(b, l // t),
            in_specs=[
                pl.BlockSpec((1, e, cap, d), lambda bb, i, st: (bb, 0, 0, 0)),
                pl.BlockSpec((1, t, e), lambda bb, i, st: (bb, i, 0)),
                pl.BlockSpec((1, t, LANE), lambda bb, i, st: (bb, i, 0)),
                pl.BlockSpec((LANE, e * win), lambda bb, i, st: (0, 0)),
                pl.BlockSpec((1, t, LANE), lambda bb, i, st: (bb, i, 0)),
                pl.BlockSpec((1, t, d), lambda bb, i, st: (bb, i, 0)),
                pl.BlockSpec((1, 1, 6, d), lambda bb, i, st: (layer, mrow(bb), 0, 0)),
            ],
            out_specs=pl.BlockSpec((1, t, d), lambda bb, i, st: (bb, i, 0)),
        ),
        out_shape=jax.ShapeDtypeStruct((b, l, d), F32),
        compiler_params=_cp("arbitrary", "arbitrary"),
        name=name,
    )(starts, ye, pos_t, pos_tf, rep, aff, h2, mod)


_PERM64 = tuple(list(range(0, 16)) + list(range(32, 48)) + list(range(16, 32)) + list(range(48, 64)))


def _rope_tables(seq_len):
    t = jnp.arange(seq_len)
    row = (t // GRID_W).astype(F32)
    col = (t % GRID_W).astype(F32)
    n_freq = C_HEAD_DIM // 4
    inv = ROPE_THETA ** (-jnp.arange(n_freq, dtype=F32) / n_freq)
    ar, ac = row[:, None] * inv[None], col[:, None] * inv[None]
    c64 = jnp.concatenate([jnp.cos(ar), jnp.cos(ac), jnp.cos(ar), jnp.cos(ac)], axis=-1)
    s64 = jnp.concatenate([-jnp.sin(ar), -jnp.sin(ac), jnp.sin(ar), jnp.sin(ac)], axis=-1)
    reps = C_QK // C_HEAD_DIM
    return jnp.tile(c64, (1, reps)), jnp.tile(s64, (1, reps))


def kernel(x, c, ctx, c_ctx, w_ada, b_ada, g_mix, g_ffn, w_in_ab, conv_a, conv_b, conv_b_bias, ln_b_g, ln_b_b, w_out_ab, w_in_cd, g_q, g_k, lam_q1, lam_k1, lam_q2, lam_k2, g_subln, conv_d, hf_w1, hf_b1, hf_freq, hf_w2, hf_b2, hf_w3, hf_bias, w_out_cd, w_router, w_gate, w_up, w_down):
    bsz, seq_lat, d = x.shape
    seq_ctx = ctx.shape[1]
    cap_lat = max(1, (EC_CAPACITY * seq_lat) // N_EXPERTS)
    cap_ctx = max(1, (EC_CAPACITY * seq_ctx) // N_EXPERTS)

    cond = jnp.zeros((MOD_ROWS, d), F32).at[:bsz].set(c).at[CTX_ROW].set(c_ctx)
    mod = _ada(cond, w_ada, b_ada)

    perm = jnp.asarray(_PERM64)

    def permute_qk(w):
        quarter = C_HEAD_DIM // 4
        qk = w[:, :2 * C_QK].reshape(d, 2 * C_QK // C_HEAD_DIM, 2, 2, quarter)
        return jnp.concatenate([jnp.swapaxes(qk, 2, 3).reshape(d, 2 * C_QK), w[:, 2 * C_QK:]], axis=1)

    gi = jnp.arange(C_QK) // C_HEAD_DIM
    gsum = (gi[:, None] == gi[None, :]).astype(BF16)
    cos_t, sin_t = _rope_tables(seq_lat)
    dft_lat, dft_ctx = _dft_mats(seq_lat), _dft_mats(seq_ctx)

    flat = lambda a: a.reshape(1, bsz * seq_ctx, a.shape[-1])

    def proj_in_ctx(h, *args, **kwargs):
        return _proj_in(flat(h), *args, **kwargs).reshape(bsz, seq_ctx, -1)

    h_lat, h_ctx = x, ctx
    for l in range(DEPTH):
        last = l == DEPTH - 1
        odd = l % 2 == 1
        i = l // 2
        gm = g_mix[l][None]
        if not odd:
            w_in = w_in_ab[i].astype(BF16)
            w_out = w_out_ab[i].astype(BF16)
            cargs = (conv_a[i], conv_b[i], conv_b_bias[i], ln_b_g[i], ln_b_b[i])
            p_lat = _proj_in(h_lat, mod, l, None, gm, w_in, name=f"proj_in_lat{l}")
            ys_lat = [_conv_mix(p_lat, *cargs, name=f"conv_mix_lat{l}")]
            if not last:
                p_ctx = proj_in_ctx(h_ctx, mod, l, CTX_ROW, gm, w_in, name=f"proj_in_ctx{l}")
                ys_ctx = [_conv_mix(p_ctx, *cargs, name=f"conv_mix_ctx{l}")]
        else:
            lam_init = 0.8 - 0.6 * math.exp(-0.3 * l)
            w_in = permute_qk(w_in_cd[i]).astype(BF16)
            w_out = w_out_cd[i].astype(BF16)
            gqk = jnp.stack([jnp.tile(g_q[i][perm], C_QK // C_HEAD_DIM) * (C_HEAD_DIM ** -0.5 * math.log2(math.e)),
                             jnp.tile(g_k[i][perm], C_QK // C_HEAD_DIM)])
            lamv = jnp.stack([lam_q1[i], lam_k1[i], lam_q2[i], lam_k2[i]])
            gsub = g_subln[i][None]
            p_lat = _proj_in(h_lat, mod, l, None, gm, w_in, norm_rows=(0, 1), gqk=gqk, gsum=gsum,
                             cos=cos_t, sin=sin_t, name=f"proj_in_lat{l}")
            if last:
                p_ctx = proj_in_ctx(h_ctx, mod, l, CTX_ROW, gm, w_in[:, C_QK:2 * C_QK + C_V], norm_rows=(1,),
                                    gqk=gqk, gsum=gsum, name=f"proj_in_ctx{l}")
                kv_off = 0
            else:
                p_ctx = proj_in_ctx(h_ctx, mod, l, CTX_ROW, gm, w_in, norm_rows=(0, 1), gqk=gqk, gsum=gsum,
                                    name=f"proj_in_ctx{l}")
                kv_off = C_HEADS
            o_lat = _attn(p_lat, p_ctx, kv_off, p_lat, lamv, gsub, lam_init, name=f"attn_lat{l}")
            fargs = (hf_w1[i], hf_b1[i], hf_freq[i], hf_w2[i], hf_b2[i], hf_w3[i])
            filt_lat = _hy_filter(seq_lat, *fargs, name=f"hy_filter_lat{l}")
            ys_lat = [o_lat, _hyena(p_lat, conv_d[i], filt_lat, hf_bias[i], dft_lat, f"lat{l}")]
            if not last:
                o_ctx = _attn(p_ctx, p_ctx, kv_off, None, lamv, gsub, lam_init, name=f"attn_ctx{l}")
                filt_ctx = _hy_filter(seq_ctx, *fargs, name=f"hy_filter_ctx{l}")
                ys_ctx = [o_ctx, _hyena(p_ctx, conv_d[i], filt_ctx, hf_bias[i], dft_ctx, f"ctx{l}")]

        gf = g_ffn[l][None]
        wr = jnp.pad(w_router[l], ((0, 0), (0, LANE - N_EXPERTS)))
        streams = [("lat", h_lat, ys_lat, None, cap_lat)]
        if not last:
            streams.append(("ctx", h_ctx, ys_ctx, CTX_ROW, cap_ctx))
        staged = []
        for tag, h, ys, mrow, cap in streams:
            if mrow is None:
                h2, u2, aff, afft = _proj_out(ys, w_out, h, mod, l, mrow, gf, wr, name=f"proj_out_{tag}{l}")
            else:
                h2, u2, aff, afft = _proj_out([flat(y) for y in ys], w_out, flat(h), mod, l, mrow, gf, wr,
                                              name=f"proj_out_{tag}{l}")
                h2, u2, aff = (a.reshape(bsz, seq_ctx, a.shape[-1]) for a in (h2, u2, aff))
                afft = jnp.transpose(afft.reshape(N_EXPERTS, bsz, seq_ctx), (1, 0, 2))
            tc = min(MOE_CHUNK, h.shape[1])
            win = min(MOE_WINDOW, cap)
            pos, starts = _route(afft, cap, tc, name=f"route_{tag}{l}")
            xe = _gather(u2, pos, starts, cap, tc, win, name=f"gather_{tag}{l}")
            staged.append((tag, h2, aff, pos, starts, xe, mrow, tc, win))
        yes = _ffn([s[5] for s in staged], w_gate, w_up, w_down, l, name=f"ffn{l}")
        outs = []
        for (tag, h2, aff, pos, starts, _, mrow, tc, win), ye in zip(staged, yes):
            pos_t = jnp.transpose(pos, (0, 2, 1))
            outs.append(_combine(ye, pos_t, aff, starts, h2, mod, l, mrow, tc, win, name=f"combine_{tag}{l}"))
        h_lat = outs[0]
        if not last:
            h_ctx = outs[1]
    return h_lat
```

```python
import functools
import math

import jax
import jax.numpy as jnp
from jax import lax
from jax.experimental import pallas as pl
from jax.experimental.pallas import tpu as pltpu

F32 = jnp.float32
BF16 = jnp.bfloat16

D_MODEL = 1024
DEPTH = 4
GRID_W = 64
A_WIDTH = 512
B_WIDTH = 512
SHORT_CONV_W = 3
CONFORMER_CONV_W = 31
AB_IN = 3 * A_WIDTH + 2 * B_WIDTH
C_HEAD_DIM = 64
C_V_DIM = 128
C_HEADS = 4
C_QK = 512
C_V = 512
HY_WIDTH = 512
CD_IN = 2 * C_QK + C_V + 3 * HY_WIDTH
ROPE_THETA = 10000.0
HY_BANDS = 16
HY_EMB = 2 * HY_BANDS + 1
HY_MIN_DECAY = math.log(1e-2) / 1.5
HY_MAX_DECAY = math.log(1e-2) / 0.3
N_EXPERTS = 16
EXPERT_FF = 1024
EC_CAPACITY = 2
NORM_EPS = 1e-6
SUBLN_EPS = 1e-5

LANE = 128
SUBLANE = 8
HALO = 16
CONV_ROWS = 128
ROW_ALIGN = 16
MOE_CHUNK = 256
MOE_WINDOW = 64
ATTN_ROWS = 128
ROW_CHUNK = 256
MOD_ROWS = 16
CTX_ROW = 8
F32_MIN_NORMAL_BITS = 0x00800000

ROW_TILE = 1024
CONV_TILE = 512
FREQ_TILE = 512
PROJ_COLS = 512
ADA_COLS = 1536
FFN_SAMPLES = 4
FFN_COLS = 512
VMEM_LIMIT = 52 * 1024 * 1024
FFN_VMEM_LIMIT = 56 * 1024 * 1024


def _cp(*sem, vmem=VMEM_LIMIT):
    return pltpu.CompilerParams(dimension_semantics=sem, vmem_limit_bytes=vmem)


def _dot(a, b):
    return jnp.dot(a, b, preferred_element_type=F32)


def _dot3(a, b):
    a_hi = a.astype(BF16)
    a_lo = (a - a_hi.astype(F32)).astype(BF16)
    b_hi = b.astype(BF16)
    b_lo = (b - b_hi.astype(F32)).astype(BF16)
    return _dot(a_hi, b_hi) + (_dot(a_hi, b_lo) + _dot(a_lo, b_hi))


def _rms_mod(h, g, shift, scale):
    ms = jnp.mean(h * h, axis=-1, keepdims=True)
    return (h * lax.rsqrt(ms + NORM_EPS) * g) * (1.0 + scale) + shift


def _ada_kernel(c_ref, w_ref, b_ref, o_ref):
    c = c_ref[...]
    s = (c * jax.nn.sigmoid(c)).astype(BF16)
    o_ref[0] = _dot(s, w_ref[0].astype(BF16)) + b_ref[0]


def _ada(cond, w_ada, b_ada):
    depth, d, n = w_ada.shape
    tn = ADA_COLS
    out = pl.pallas_call(
        _ada_kernel,
        grid=(depth, n // tn),
        in_specs=[
            pl.BlockSpec((MOD_ROWS, d), lambda l, j: (0, 0)),
            pl.BlockSpec((1, d, tn), lambda l, j: (l, 0, j)),
            pl.BlockSpec((1, 1, tn), lambda l, j: (l, 0, j)),
        ],
        out_specs=pl.BlockSpec((1, MOD_ROWS, tn), lambda l, j: (l, 0, j)),
        out_shape=jax.ShapeDtypeStruct((depth, MOD_ROWS, n), F32),
        compiler_params=_cp("arbitrary", "arbitrary"),
        name="ada",
    )(cond, w_ada, b_ada.reshape(depth, 1, n))
    return out.reshape(depth, MOD_ROWS, 6, d)


def _swap32(x):
    n = x.shape[-1]
    lane = lax.broadcasted_iota(jnp.int32, x.shape, x.ndim - 1)
    up = pltpu.roll(x, n - 32, x.ndim - 1)
    dn = pltpu.roll(x, 32, x.ndim - 1)
    return jnp.where((lane & 32) == 0, up, dn)


def _proj_in_kernel(*refs, n_out, chunk, norm_rows, rope):
    h_ref, mod_ref, g_ref, w_ref = refs[:4]
    k = 4
    if norm_rows:
        gqk_ref, gsum_ref = refs[k:k + 2]
        k += 2
    if rope:
        cos_ref, sin_ref = refs[k:k + 2]
        k += 2
    o_ref, u_scr = refs[k], refs[k + 1]
    mod = mod_ref[0, 0]
    u = _rms_mod(h_ref[0], g_ref[...], mod[0:1], mod[1:2])
    u_scr[...] = u.astype(BF16)
    for n in range(n_out // chunk):
        p = _dot(u_scr[...], w_ref[:, n * chunk:(n + 1) * chunk])
        if n < len(norm_rows):
            ss = _dot((p * p).astype(BF16), gsum_ref[...])
            r = norm_rows[n]
            p = p * lax.rsqrt(ss * (1.0 / C_HEAD_DIM) + NORM_EPS) * gqk_ref[r:r + 1, :]
            if rope:
                p = p * cos_ref[...] + _swap32(p) * sin_ref[...]
        o_ref[0, :, n * chunk:(n + 1) * chunk] = p.astype(o_ref.dtype)


def _proj_in(h, mod, layer, mod_row, g, w, *, norm_rows=(), gqk=None, gsum=None, cos=None, sin=None, name):
    b, l, d = h.shape
    n_out = w.shape[1]
    t = min(ROW_TILE, l)
    chunk = PROJ_COLS
    rope = cos is not None
    mrow = (lambda bb: bb) if mod_row is None else (lambda bb: mod_row)
    in_specs = [
        pl.BlockSpec((1, t, d), lambda i, bb: (bb, i, 0)),
        pl.BlockSpec((1, 1, 6, d), lambda i, bb: (layer, mrow(bb), 0, 0)),
        pl.BlockSpec((1, d), lambda i, bb: (0, 0)),
        pl.BlockSpec((d, n_out), lambda i, bb: (0, 0)),
    ]
    args = [h, mod, g, w]
    if norm_rows:
        in_specs += [pl.BlockSpec(gqk.shape, lambda i, bb: (0, 0)),
                     pl.BlockSpec(gsum.shape, lambda i, bb: (0, 0))]
        args += [gqk, gsum]
    if rope:
        in_specs += [pl.BlockSpec((t, chunk), lambda i, bb: (i, 0))] * 2
        args += [cos, sin]
    return pl.pallas_call(
        functools.partial(_proj_in_kernel, n_out=n_out, chunk=chunk, norm_rows=tuple(norm_rows), rope=rope),
        grid=(l // t, b),
        in_specs=in_specs,
        out_specs=pl.BlockSpec((1, t, n_out), lambda i, bb: (bb, i, 0)),
        out_shape=jax.ShapeDtypeStruct((b, l, n_out), BF16),
        scratch_shapes=[pltpu.VMEM((t, d), BF16)],
        compiler_params=_cp("arbitrary", "arbitrary"),
        name=name,
    )(*args)


def _halo_specs(t, l, width, col):
    r = t // HALO
    last = l // HALO - 1
    return [
        pl.BlockSpec((1, t, width), lambda bb, i: (bb, i, col)),
        pl.BlockSpec((1, HALO, width), lambda bb, i: (bb, jnp.maximum(i * r - 1, 0), col)),
        pl.BlockSpec((1, HALO, width), lambda bb, i: (bb, jnp.minimum((i + 1) * r, last), col)),
    ]


def _conv_mix_kernel(pc_ref, pp_ref, pn_ref, ca_ref, cb_ref, bb_ref, lg_ref, lb_ref, o_ref,
                     abuf, gbuf, zbuf, sbuf, wbuf, *, t, nt):
    i = pl.program_id(1)
    keep_prev = jnp.where(i == 0, 0.0, 1.0)
    keep_next = jnp.where(i == nt - 1, 0.0, 1.0)

    def prods(p):
        p = p.astype(F32)
        a = p[:, A_WIDTH:2 * A_WIDTH] * p[:, 2 * A_WIDTH:3 * A_WIDTH]
        g = p[:, 3 * A_WIDTH:3 * A_WIDTH + B_WIDTH] * jax.nn.sigmoid(p[:, 3 * A_WIDTH + B_WIDTH:])
        return a, g

    n_slab = B_WIDTH // LANE

    def put_g(r0, n, g):
        for c in range(n_slab):
            gbuf[c, r0:r0 + n, :] = g[:, c * LANE:(c + 1) * LANE]

    a, g = prods(pp_ref[0])
    abuf[0:HALO] = a * keep_prev
    put_g(0, HALO, g * keep_prev)
    a, g = prods(pc_ref[0])
    abuf[HALO:HALO + t] = a
    put_g(HALO, t, g)
    a, g = prods(pn_ref[0])
    abuf[HALO + t:2 * HALO + t] = a * keep_next
    put_g(HALO + t, HALO, g * keep_next)

    ca = ca_ref[...]
    ya = (ca[0:1] * abuf[HALO - 1:HALO - 1 + t] + ca[1:2] * abuf[HALO:HALO + t]
          + ca[2:3] * abuf[HALO + 1:HALO + 1 + t])
    o_ref[0, :, 0:A_WIDTH] = (pc_ref[0, :, 0:A_WIDTH].astype(F32) * ya).astype(o_ref.dtype)

    pad = (CONFORMER_CONV_W - 1) // 2
    rb = CONV_ROWS
    nv = rb // SUBLANE
    tail = gbuf.shape[1] - 2 * HALO - t
    put_g(2 * HALO + t, tail, jnp.zeros((tail, B_WIDTH), F32))

    def shift_rows(i, carry):
        base = pl.multiple_of(i * rb, rb)
        for c in range(n_slab):
            x = gbuf[c, pl.ds(base, rb + SUBLANE), :]
            for r in range(1, SUBLANE):
                sbuf[r - 1, c, pl.ds(base, rb), :] = x[r:r + rb]
        return carry

    lax.fori_loop(0, pl.cdiv(t + 2 * HALO - SUBLANE, rb), shift_rows, 0)
    for k in range(CONFORMER_CONV_W):
        wbuf[k] = jnp.broadcast_to(cb_ref[k:k + 1, :], (SUBLANE, B_WIDTH))

    def conv_rows(i, carry):
        base = pl.multiple_of(i * rb, rb)
        for c in range(n_slab):
            c0 = c * LANE
            acc = jnp.zeros((nv, SUBLANE, LANE), F32) + bb_ref[:, c0:c0 + LANE]
            for k in range(CONFORMER_CONV_W):
                s = HALO - pad + k
                r = s % SUBLANE
                src = gbuf.at[c] if r == 0 else sbuf.at[r - 1, c]
                rows = src[pl.ds(base + (s - r), rb), :].reshape(nv, SUBLANE, LANE)
                acc = acc + wbuf[k, :, c0:c0 + LANE] * rows
            zbuf[pl.ds(base, rb), c0:c0 + LANE] = acc.reshape(rb, LANE)
        return carry

    lax.fori_loop(0, t // rb, conv_rows, 0)
    z = zbuf[...]
    mu = jnp.mean(z, axis=-1, keepdims=True)
    zc = z - mu
    var = jnp.mean(zc * zc, axis=-1, keepdims=True)
    y = zc * lax.rsqrt(var + NORM_EPS) * lg_ref[...] + lb_ref[...]
    o_ref[0, :, A_WIDTH:] = (y * jax.nn.sigmoid(y)).astype(o_ref.dtype)


def _conv_mix(p, conv_a, conv_b, conv_b_bias, ln_g, ln_b, *, name):
    b, l, n = p.shape
    t = min(CONV_TILE, l)
    nt = l // t
    full = lambda shape: pl.BlockSpec(shape, lambda bb, i: (0,) * len(shape))
    return pl.pallas_call(
        functools.partial(_conv_mix_kernel, t=t, nt=nt),
        grid=(b, nt),
        in_specs=_halo_specs(t, l, n, 0) + [
            full(conv_a.shape), full(conv_b.shape), full((1, B_WIDTH)), full((1, B_WIDTH)), full((1, B_WIDTH))],
        out_specs=pl.BlockSpec((1, t, A_WIDTH + B_WIDTH), lambda bb, i: (bb, i, 0)),
        out_shape=jax.ShapeDtypeStruct((b, l, A_WIDTH + B_WIDTH), BF16),
        scratch_shapes=[pltpu.VMEM((t + 2 * HALO, A_WIDTH), F32),
                        pltpu.VMEM((B_WIDTH // LANE, t + 2 * HALO + CONV_ROWS, LANE), F32),
                        pltpu.VMEM((t, B_WIDTH), F32),
                        pltpu.VMEM((SUBLANE - 1, B_WIDTH // LANE, t + 2 * HALO + CONV_ROWS, LANE), F32),
                        pltpu.VMEM((CONFORMER_CONV_W, SUBLANE, B_WIDTH), F32)],
        compiler_params=_cp("arbitrary", "arbitrary"),
        name=name,
    )(p, p, p, conv_a, conv_b, conv_b_bias[None], ln_g[None], ln_b[None])


def _attn_kernel(*refs, lam_init, has_lat):
    if has_lat:
        q_ref, kc_ref, vc_ref, kl_ref, vl_ref, lam_ref, gs_ref, o_ref = refs
    else:
        q_ref, kc_ref, vc_ref, lam_ref, gs_ref, o_ref = refs
    lv = lam_ref[...]
    lam = (jnp.exp(jnp.sum(lv[0:1] * lv[1:2], axis=-1, keepdims=True))
           - jnp.exp(jnp.sum(lv[2:3] * lv[3:4], axis=-1, keepdims=True)) + lam_init)
    q = q_ref[0]
    lane = lax.broadcasted_iota(jnp.int32, q.shape, 1)
    nt = (((1,), (1,)), ((), ()))
    ones_col = lambda v: jnp.concatenate([v, jnp.ones_like(v)], axis=1)
    vc = ones_col(vc_ref[0])
    if has_lat:
        vl = ones_col(vl_ref[0])
    halves = []
    tq = q.shape[0]
    rc = min(ATTN_ROWS, tq)
    for j in range(2):
        qj = jnp.where((lane >= C_HEAD_DIM) == (j == 1), q, jnp.zeros_like(q))
        outs = []
        for r0 in range(0, tq, rc):
            qr = qj[r0:r0 + rc]
            s_c = lax.dot_general(qr, kc_ref[0], nt, preferred_element_type=F32)
            m = jnp.max(s_c, axis=-1, keepdims=True)
            if has_lat:
                s_l = lax.dot_general(qr, kl_ref[0], nt, preferred_element_type=F32)
                m = jnp.maximum(m, jnp.max(s_l, axis=-1, keepdims=True))
            oj = _dot(jnp.exp2(s_c - m).astype(BF16), vc)
            if has_lat:
                oj = oj + _dot(jnp.exp2(s_l - m).astype(BF16), vl)
            outs.append(oj[:, 0:C_V_DIM] * (1.0 / oj[:, C_V_DIM:C_V_DIM + 1]))
        halves.append(jnp.concatenate(outs, axis=0))
    o = halves[0] - lam * halves[1]
    ms = jnp.mean(o * o, axis=-1, keepdims=True)
    o = o * lax.rsqrt(ms + SUBLN_EPS) * gs_ref[...] * (1.0 - lam_init)
    o_ref[0] = o.astype(o_ref.dtype)


def _attn(pq, pkv_ctx, kv_off, p_lat, lamv, g_sub, lam_init, *, name):
    b, lq, _ = pq.shape
    lc = pkv_ctx.shape[1]
    tq = min(ROW_TILE, lq)
    has_lat = p_lat is not None
    in_specs = [
        pl.BlockSpec((1, tq, LANE), lambda bb, h, i: (bb, i, h)),
        pl.BlockSpec((1, lc, LANE), lambda bb, h, i: (bb, 0, kv_off + h)),
        pl.BlockSpec((1, lc, LANE), lambda bb, h, i: (bb, 0, kv_off + C_HEADS + h)),
    ]
    args = [pq, pkv_ctx, pkv_ctx]
    if has_lat:
        ll = p_lat.shape[1]
        in_specs += [pl.BlockSpec((1, ll, LANE), lambda bb, h, i: (bb, 0, C_HEADS + h)),
                     pl.BlockSpec((1, ll, LANE), lambda bb, h, i: (bb, 0, 2 * C_HEADS + h))]
        args += [p_lat, p_lat]
    in_specs += [pl.BlockSpec((4, C_HEAD_DIM), lambda bb, h, i: (0, 0)),
                 pl.BlockSpec((1, C_V_DIM), lambda bb, h, i: (0, 0))]
    args += [lamv, g_sub]
    return pl.pallas_call(
        functools.partial(_attn_kernel, lam_init=lam_init, has_lat=has_lat),
        grid=(b, C_HEADS, lq // tq),
        in_specs=in_specs,
        out_specs=pl.BlockSpec((1, tq, LANE), lambda bb, h, i: (bb, i, h)),
        out_shape=jax.ShapeDtypeStruct((b, lq, C_V), BF16),
        compiler_params=_cp("arbitrary", "arbitrary", "arbitrary"),
        name=name,
    )(*args)


def _hy_pre_kernel(pc_ref, pp_ref, pn_ref, cd_ref, o_ref, buf, *, t, nt):
    i = pl.program_id(1)
    buf[0:HALO] = pp_ref[0].astype(F32) * jnp.where(i == 0, 0.0, 1.0)
    buf[HALO:HALO + t] = pc_ref[0].astype(F32)
    buf[HALO + t:2 * HALO + t] = pn_ref[0].astype(F32) * jnp.where(i == nt - 1, 0.0, 1.0)
    cd = cd_ref[...]
    z = (cd[0:1] * buf[HALO - 1:HALO - 1 + t] + cd[1:2] * buf[HALO:HALO + t]
         + cd[2:3] * buf[HALO + 1:HALO + 1 + t])
    o_ref[0, :, 0:HY_WIDTH] = z[:, 0:HY_WIDTH].astype(o_ref.dtype)
    o_ref[0, :, HY_WIDTH:] = (z[:, HY_WIDTH:2 * HY_WIDTH] * z[:, 2 * HY_WIDTH:]).astype(o_ref.dtype)


def _hy_pre(p, conv_d, *, name):
    b, l, _ = p.shape
    t = min(CONV_TILE, l)
    nt = l // t
    w = 3 * HY_WIDTH
    return pl.pallas_call(
        functools.partial(_hy_pre_kernel, t=t, nt=nt),
        grid=(b, nt),
        in_specs=_halo_specs(t, l, w, 1) + [pl.BlockSpec(conv_d.shape, lambda bb, i: (0, 0))],
        out_specs=pl.BlockSpec((1, t, 2 * HY_WIDTH), lambda bb, i: (bb, i, 0)),
        out_shape=jax.ShapeDtypeStruct((b, l, 2 * HY_WIDTH), BF16),
        scratch_shapes=[pltpu.VMEM((t + 2 * HALO, w), F32)],
        compiler_params=_cp("arbitrary", "arbitrary"),
        name=name,
    )(p, p, p, conv_d)


def _hy_filter_kernel(z_ref, w1_ref, b1_ref, fr_ref, w2_ref, b2_ref, w3_ref, dist_ref, dec_ref, o_ref):
    fr = fr_ref[...]
    hid = jnp.sin(fr * (_dot3(z_ref[...], w1_ref[...]) + b1_ref[...]))
    hid = jnp.sin(fr * (_dot3(hid, w2_ref[...]) + b2_ref[...]))
    h = _dot3(hid, w3_ref[...])
    o_ref[...] = h * jnp.exp(-dist_ref[...] * dec_ref[...])


def _hy_filter(seq_len, w1, b1, freq, w2, b2, w3, *, name):
    t = jnp.arange(seq_len, dtype=F32)
    t_unit = jnp.linspace(0.0, 1.0, seq_len)[:, None]
    bands = jnp.linspace(1e-4, HY_BANDS - 1, HY_BANDS)
    ang = (2.0 * math.pi / seq_len) * t[:, None] * bands[None]
    z = jnp.concatenate([t_unit, jnp.cos(ang), -jnp.sin(ang)], axis=-1)
    centre = seq_len // 2
    dist = (jnp.abs(t - centre) / max(centre, 1))[:, None]
    decay = jnp.abs(jnp.linspace(HY_MIN_DECAY, HY_MAX_DECAY, HY_WIDTH))[None]
    pad_c = lambda a, n: jnp.pad(a, ((0, 0), (0, n - a.shape[1])))
    pad_r = lambda a, n: jnp.pad(a, ((0, n - a.shape[0]), (0, 0)))
    args = [pad_c(z, LANE), pad_c(pad_r(w1, LANE), LANE), pad_c(b1[None], LANE), pad_c(freq[None], LANE),
            pad_c(pad_r(w2, LANE), LANE), pad_c(b2[None], LANE), pad_r(w3, LANE), dist, decay]
    return pl.pallas_call(
        _hy_filter_kernel,
        out_shape=jax.ShapeDtypeStruct((seq_len, HY_WIDTH), F32),
        compiler_params=pltpu.CompilerParams(vmem_limit_bytes=VMEM_LIMIT),
        name=name,
    )(*args)


def _dft_mats(l):
    m = 3 * l // 2
    hh = m // 2
    f = jnp.arange(hh, dtype=jnp.int32)
    s = jnp.arange(l, dtype=jnp.int32)
    n = s + l // 2

    def cos_sin(rows, n_cols):
        w = 64
        phase = lambda k: (2.0 * math.pi / m) * ((rows[:, None] * k[None, :]) % m).astype(F32)
        a = phase(jnp.arange(0, n_cols, w, dtype=jnp.int32))[:, :, None]
        b = phase(jnp.arange(w, dtype=jnp.int32))[:, None, :]
        ca, sa, cb, sb = jnp.cos(a), jnp.sin(a), jnp.cos(b), jnp.sin(b)
        shape = (rows.shape[0], n_cols)
        return (ca * cb - sa * sb).reshape(shape), (sa * cb + ca * sb).reshape(shape)

    c, sn = cos_sin(f, l)
    nyq = jnp.where(s % 2 == 0, 1.0, -1.0).astype(F32)
    fw = jnp.stack([c, jnp.where(f[:, None] == 0, nyq[None, :], -sn)]).astype(BF16)
    c, sn = cos_sin(n, hh)
    nyq_n = jnp.where(n % 2 == 0, 1.0, -1.0).astype(F32)
    iv_r = jnp.where(f[None, :] == 0, 1.0, 2.0 * c)
    iv_i = jnp.where(f[None, :] == 0, nyq_n[:, None], -2.0 * sn)
    iv = jnp.concatenate([iv_r, iv_i], axis=1).astype(BF16)
    return fw, iv


def _dft_fwd_kernel(*refs, tf):
    fw_ref, v_ref = refs[:2]
    o_ref = refs[-1]
    v = v_ref[0].astype(BF16)
    ur = _dot(fw_ref[0], v)
    ui = _dot(fw_ref[1], v)
    if len(refs) == 3:
        o_ref[0, 0] = ur.astype(o_ref.dtype)
        o_ref[0, 1] = ui.astype(o_ref.dtype)
        return
    h_ref = refs[2]
    hr = h_ref[0]
    hi = h_ref[1]
    row = lax.broadcasted_iota(jnp.int32, ur.shape, 0) + pl.program_id(0) * tf
    packed = row == 0
    uihi = ui * hi
    o_ref[0, 0] = (ur * hr - jnp.where(packed, 0.0, uihi)).astype(o_ref.dtype)
    o_ref[0, 1] = jnp.where(packed, uihi, ur * hi + ui * hr).astype(o_ref.dtype)


def _dft_fwd(fw, v, col, hspec, out_dtype, *, name):
    b, l, _ = v.shape
    hh = fw.shape[1]
    tf = min(FREQ_TILE, hh)
    in_specs = [
        pl.BlockSpec((2, tf, l), lambda i, bb: (0, i, 0)),
        pl.BlockSpec((1, l, HY_WIDTH), lambda i, bb: (bb, 0, col)),
    ]
    args = [fw, v]
    if hspec is not None:
        in_specs.append(pl.BlockSpec((2, tf, HY_WIDTH), lambda i, bb: (0, i, 0)))
        args.append(hspec)
    return pl.pallas_call(
        functools.partial(_dft_fwd_kernel, tf=tf),
        grid=(hh // tf, b),
        in_specs=in_specs,
        out_specs=pl.BlockSpec((1, 2, tf, HY_WIDTH), lambda i, bb: (bb, 0, i, 0)),
        out_shape=jax.ShapeDtypeStruct((b, 2, hh, HY_WIDTH), out_dtype),
        compiler_params=_cp("arbitrary", "arbitrary"),
        name=name,
    )(*args)


def _dft_inv_kernel(iv_ref, y_ref, gv_ref, bias_ref, o_ref, *, inv_m):
    m = iv_ref.shape[1]
    yhat = y_ref[0].reshape(m, HY_WIDTH)
    conv = _dot(iv_ref[...], yhat) * inv_m
    gv = gv_ref[0].astype(F32)
    o_ref[0] = (gv[:, 0:HY_WIDTH] * (conv + gv[:, HY_WIDTH:] * bias_ref[...])).astype(o_ref.dtype)


def _dft_inv(iv, yhat, gv, bias, *, name):
    b, l, _ = gv.shape
    m = iv.shape[1]
    tn = min(ROW_TILE, l)
    return pl.pallas_call(
        functools.partial(_dft_inv_kernel, inv_m=1.0 / m),
        grid=(l // tn, b),
        in_specs=[
            pl.BlockSpec((tn, m), lambda i, bb: (i, 0)),
            pl.BlockSpec((1, 2, m // 2, HY_WIDTH), lambda i, bb: (bb, 0, 0, 0)),
            pl.BlockSpec((1, tn, 2 * HY_WIDTH), lambda i, bb: (bb, i, 0)),
            pl.BlockSpec((1, HY_WIDTH), lambda i, bb: (0, 0)),
        ],
        out_specs=pl.BlockSpec((1, tn, HY_WIDTH), lambda i, bb: (bb, i, 0)),
        out_shape=jax.ShapeDtypeStruct((b, l, HY_WIDTH), BF16),
        compiler_params=_cp("arbitrary", "arbitrary"),
        name=name,
    )(iv, yhat, gv, bias)


def _hyena(p, conv_d, filt, bias, dft, tag):
    fw, iv = dft
    fhat = _dft_fwd(fw, filt[None], 0, None, F32, name="hy_fhat_" + tag)[0]
    gv = _hy_pre(p, conv_d, name="hy_pre_" + tag)
    yhat = _dft_fwd(fw, gv, 1, fhat, BF16, name="hy_fwd_" + tag)
    return _dft_inv(iv, yhat, gv, bias[None], name="hy_inv_" + tag)


def _proj_out_kernel(*refs, n_y):
    y_refs = refs[:n_y]
    w_ref, h_ref, mod_ref, g_ref, wr_ref, h2_ref, u2_ref, aff_ref, afft_ref = refs[n_y:]
    mod = mod_ref[0, 0]
    wr = wr_ref[...]
    wr_hi = wr.astype(BF16).astype(F32)
    wr_hl = (wr_hi + pltpu.roll(wr - wr_hi, N_EXPERTS, 1)).astype(BF16)
    t = h_ref.shape[1]
    rc = min(ROW_CHUNK, t)
    for r0 in range(0, t, rc):
        acc = None
        k0 = 0
        for y_ref in y_refs:
            kk = y_ref.shape[2]
            part = _dot(y_ref[0, r0:r0 + rc, :], w_ref[k0:k0 + kk, :])
            acc = part if acc is None else acc + part
            k0 += kk
        h2 = h_ref[0, r0:r0 + rc, :] + mod[2:3] * acc
        h2_ref[0, r0:r0 + rc, :] = h2
        u2 = _rms_mod(h2, g_ref[...], mod[3:4], mod[4:5])
        u2_ref[0, r0:r0 + rc, :] = u2.astype(u2_ref.dtype)
        u_hi = u2.astype(BF16)
        u_lo = (u2 - u_hi.astype(F32)).astype(BF16)
        parts = _dot(u_hi, wr_hl) + _dot(u_lo, wr_hl)
        logits = parts + pltpu.roll(parts, LANE - N_EXPERTS, 1)
        lane = lax.broadcasted_iota(jnp.int32, logits.shape, 1)
        logits = jnp.where(lane < N_EXPERTS, logits, -jnp.inf)
        e = jnp.exp(logits - jnp.max(logits, axis=-1, keepdims=True))
        aff = e / jnp.sum(e, axis=-1, keepdims=True)
        aff_ref[0, r0:r0 + rc, :] = aff
        afft_ref[0, :, r0:r0 + rc] = aff.T[0:N_EXPERTS, :]


def _proj_out(ys, w, h, mod, layer, mod_row, g, wr, *, name):
    b, l, d = h.shape
    t = min(ROW_TILE, l)
    mrow = (lambda bb: bb) if mod_row is None else (lambda bb: mod_row)
    in_specs = [pl.BlockSpec((1, t, y.shape[2]), lambda bb, i: (bb, i, 0)) for y in ys] + [
        pl.BlockSpec(w.shape, lambda bb, i: (0, 0)),
        pl.BlockSpec((1, t, d), lambda bb, i: (bb, i, 0)),
        pl.BlockSpec((1, 1, 6, d), lambda bb, i: (layer, mrow(bb), 0, 0)),
        pl.BlockSpec((1, d), lambda bb, i: (0, 0)),
        pl.BlockSpec((d, LANE), lambda bb, i: (0, 0)),
    ]
    return pl.pallas_call(
        functools.partial(_proj_out_kernel, n_y=len(ys)),
        grid=(b, l // t),
        in_specs=in_specs,
        out_specs=[
            pl.BlockSpec((1, t, d), lambda bb, i: (bb, i, 0)),
            pl.BlockSpec((1, t, d), lambda bb, i: (bb, i, 0)),
            pl.BlockSpec((1, t, LANE), lambda bb, i: (bb, i, 0)),
            pl.BlockSpec((1, N_EXPERTS, t), lambda bb, i: (bb, 0, i)),
        ],
        out_shape=[
            jax.ShapeDtypeStruct((b, l, d), F32),
            jax.ShapeDtypeStruct((b, l, d), BF16),
            jax.ShapeDtypeStruct((b, l, LANE), F32),
            jax.ShapeDtypeStruct((b, N_EXPERTS, l), F32),
        ],
        compiler_params=_cp("arbitrary", "arbitrary"),
        name=name,
    )(*ys, w, h, mod, g, wr)


def _excl_cumsum(mask):
    r, l = mask.shape
    x = jnp.where(mask, 1.0, 0.0)
    i0 = lax.broadcasted_iota(jnp.int32, (LANE, LANE), 0)
    i1 = lax.broadcasted_iota(jnp.int32, (LANE, LANE), 1)
    tri = jnp.where(i0 < i1, 1.0, 0.0).astype(BF16)
    carry = jnp.zeros((r, 1), F32)
    outs = []
    for c in range(l // LANE):
        xc = x[:, c * LANE:(c + 1) * LANE]
        outs.append(_dot(xc.astype(BF16), tri) + carry)
        carry = carry + jnp.sum(xc, axis=-1, keepdims=True)
    return jnp.concatenate(outs, axis=1)


def _route_kernel(a_ref, pos_ref, starts_ref, *, cap, tc):
    aff = a_ref[...]
    e, l = aff.shape
    capf = jnp.float32(cap)
    count = lambda mask: jnp.sum(jnp.where(mask, 1.0, 0.0), axis=-1, keepdims=True)

    def body(i, thr):
        cand = thr | jnp.left_shift(jnp.int32(1), 30 - i)
        return jnp.where(count(aff >= pltpu.bitcast(cand, F32)) >= capf, cand, thr)

    thr = lax.fori_loop(0, 31, body, jnp.zeros((e, 1), jnp.int32))
    t_lo = pltpu.bitcast(thr, F32)
    t_hi = pltpu.bitcast(jnp.maximum(thr + 1, F32_MIN_NORMAL_BITS), F32)
    above = aff >= t_hi
    sel0 = jnp.where(above, 1.0, 0.0)
    mid0 = jnp.where((aff >= t_lo) & jnp.logical_not(above), 1.0, 0.0)
    lane = lax.broadcasted_iota(jnp.int32, aff.shape, 1).astype(F32)

    def fill(carry):
        sel, mid, need = carry
        v = jnp.where(mid > 0.0, aff, -1.0)
        best = jnp.max(v, axis=-1, keepdims=True)
        first = jnp.min(jnp.where(v == best, lane, jnp.float32(l)), axis=-1, keepdims=True)
        pick = (lane == first) & (need > 0.0)
        return (jnp.where(pick, 1.0, sel), jnp.where(pick, 0.0, mid), need - jnp.where(need > 0.0, 1.0, 0.0))

    sel, _, _ = lax.while_loop(lambda carry: jnp.max(carry[2]) > 0.0, fill, (sel0, mid0, capf - count(above)))
    chosen = sel > 0.0
    pos = _excl_cumsum(chosen)
    pos_ref[...] = jnp.where(chosen, pos.astype(jnp.int32), -1)
    chunk = lax.broadcasted_iota(jnp.int32, (e, LANE), 1)
    starts = jnp.zeros((e, LANE), F32)
    for c in range(l // tc):
        starts = starts + jnp.where(chunk > c, count(chosen[:, c * tc:(c + 1) * tc]), 0.0)
    starts_ref[...] = starts.astype(jnp.int32)


def _route(afft, cap, tc, *, name):
    b, e, l = afft.shape
    pos, starts = pl.pallas_call(
        functools.partial(_route_kernel, cap=cap, tc=tc),
        out_shape=[jax.ShapeDtypeStruct((b * e, l), jnp.int32),
                   jax.ShapeDtypeStruct((b * e, LANE), jnp.int32)],
        compiler_params=pltpu.CompilerParams(vmem_limit_bytes=VMEM_LIMIT),
        name=name,
    )(afft.reshape(b * e, l))
    return pos.reshape(b, e, l), starts[:, :l // tc + 1].reshape(-1)


def _windows(st_ref, b, c, n_e, n_chunks, win):
    starts, n_win = [], None
    for e in range(n_e):
        k = (b * n_e + e) * (n_chunks + 1) + c
        start = st_ref[k] & -ROW_ALIGN
        need = (st_ref[k + 1] - start + (win - 1)) // win
        starts.append(start)
        n_win = need if n_win is None else jnp.maximum(n_win, need)
    return starts, n_win


def _window(starts, e, w, cap, win):
    first = starts[e] + w * win
    return first, pl.multiple_of(jnp.minimum(first, cap - win), ROW_ALIGN)


def _gather_kernel(st_ref, u_ref, pos_ref, o_ref, *, cap, tc, win):
    b = pl.program_id(0)
    n_e = pos_ref.shape[1]
    n_chunks = u_ref.shape[1] // tc
    o_ref[...] = jnp.zeros(o_ref.shape, o_ref.dtype)
    slot = lax.broadcasted_iota(jnp.int32, (win, tc), 0)
    for c in range(n_chunks):
        x = u_ref[0, c * tc:(c + 1) * tc, :]
        pos = pos_ref[0, :, c * tc:(c + 1) * tc]
        starts, n_win = _windows(st_ref, b, c, n_e, n_chunks, win)

        def window(w, carry):
            blocks, at = [], []
            for e in range(n_e):
                first, row0 = _window(starts, e, w, cap, win)
                idx = slot + row0
                blocks.append(jnp.where((pos[e:e + 1] == idx) & (idx >= first), 1.0, 0.0).astype(BF16))
                at.append(row0)
            y = _dot(jnp.concatenate(blocks, axis=0), x).astype(o_ref.dtype)
            for e in range(n_e):
                rows = pl.ds(at[e], win)
                o_ref[0, e, rows, :] = o_ref[0, e, rows, :] + y[e * win:(e + 1) * win]
            return carry

        lax.fori_loop(0, n_win, window, 0)


def _gather(u2, pos, starts, cap, tc, win, *, name):
    b, l, d = u2.shape
    e = pos.shape[1]
    return pl.pallas_call(
        functools.partial(_gather_kernel, cap=cap, tc=tc, win=win),
        grid_spec=pltpu.PrefetchScalarGridSpec(
            num_scalar_prefetch=1,
            grid=(b,),
            in_specs=[
                pl.BlockSpec((1, l, d), lambda bb, st: (bb, 0, 0)),
                pl.BlockSpec((1, e, l), lambda bb, st: (bb, 0, 0)),
            ],
            out_specs=pl.BlockSpec((1, e, cap, d), lambda bb, st: (bb, 0, 0, 0)),
        ),
        out_shape=jax.ShapeDtypeStruct((b, e, cap, d), BF16),
        compiler_params=_cp("arbitrary"),
        name=name,
    )(starts, u2, pos)


def _ffn_kernel(*refs, n_x, fc):
    x_refs = refs[:n_x]
    wg_ref, wu_ref, wd_ref = refs[n_x:n_x + 3]
    y_refs = refs[n_x + 3:2 * n_x + 3]
    wg_s, wu_s, wd_s = refs[2 * n_x + 3:]

    @pl.when(pl.program_id(1) == 0)
    def _():
        wg_s[...] = wg_ref[0, 0].astype(BF16)
        wu_s[...] = wu_ref[0, 0].astype(BF16)
        wd_s[...] = wd_ref[0, 0].astype(BF16)

    d = x_refs[0].shape[3]
    rows = [r.shape[0] * r.shape[2] for r in x_refs]
    x = jnp.concatenate([r[:, 0].reshape(n, d) for r, n in zip(x_refs, rows)], axis=0)
    y = None
    for f0 in range(0, wg_s.shape[1], fc):
        a = _dot(x, wg_s[:, f0:f0 + fc])
        up = _dot(x, wu_s[:, f0:f0 + fc])
        hm = (a * jax.nn.sigmoid(a) * up).astype(BF16)
        part = _dot(hm, wd_s[f0:f0 + fc, :])
        y = part if y is None else y + part
    r0 = 0
    for y_ref, n in zip(y_refs, rows):
        y_ref[:, 0] = y[r0:r0 + n].reshape(y_ref.shape[0], y_ref.shape[2], d).astype(y_ref.dtype)
        r0 += n


def _ffn(xs, w_gate, w_up, w_down, layer, *, name):
    b, e, _, d = xs[0].shape
    f = w_gate.shape[3]
    bb = FFN_SAMPLES
    xspec = lambda x: pl.BlockSpec((bb, 1, x.shape[2], d), lambda ee, j: (j, ee, 0, 0))
    return pl.pallas_call(
        functools.partial(_ffn_kernel, n_x=len(xs), fc=FFN_COLS),
        grid=(e, b // bb),
        in_specs=[xspec(x) for x in xs] + [
            pl.BlockSpec((1, 1, d, f), lambda ee, j: (layer, ee, 0, 0)),
            pl.BlockSpec((1, 1, d, f), lambda ee, j: (layer, ee, 0, 0)),
            pl.BlockSpec((1, 1, f, d), lambda ee, j: (layer, ee, 0, 0)),
        ],
        out_specs=[xspec(x) for x in xs],
        out_shape=[jax.ShapeDtypeStruct(x.shape, BF16) for x in xs],
        scratch_shapes=[pltpu.VMEM((d, f), BF16), pltpu.VMEM((d, f), BF16), pltpu.VMEM((f, d), BF16)],
        compiler_params=_cp("arbitrary", "arbitrary", vmem=FFN_VMEM_LIMIT),
        name=name,
    )(*xs, w_gate, w_up, w_down)


def _combine_kernel(st_ref, ye_ref, ptf_ref, rep_ref, aff_ref, h_ref, mod_ref, o_ref, *, cap, tc, win, n_chunks):
    b = pl.program_id(0)
    n_e = ye_ref.shape[1]
    t = h_ref.shape[1]
    gate = mod_ref[0, 0][5:6]
    rep = rep_ref[...]
    lane = lax.broadcasted_iota(jnp.int32, (1, n_e * win), 1)
    for cc in range(t // tc):
        c = pl.program_id(1) * (t // tc) + cc
        rows = slice(cc * tc, (cc + 1) * tc)
        starts, n_win = _windows(st_ref, b, c, n_e, n_chunks, win)
        slot_of = _dot(ptf_ref[0, rows, :], rep)
        gate_of = _dot(aff_ref[0, rows, :].astype(BF16), rep)

        def window(w):
            ys = []
            want = lane % win
            valid = lane < 0
            for e in range(n_e):
                first, row0 = _window(starts, e, w, cap, win)
                ys.append(ye_ref[0, e, pl.ds(row0, win), :])
                mine = lane // win == e
                want = jnp.where(mine, want + row0, want)
                valid = valid | (mine & (lane % win + row0 >= first))
            gates = jnp.where((slot_of == want.astype(F32)) & valid, gate_of, 0.0).astype(BF16)
            return gate * _dot(gates, jnp.concatenate(ys, axis=0))

        def more(w, carry):
            o_ref[0, rows, :] = o_ref[0, rows, :] + window(w)
            return carry

        o_ref[0, rows, :] = h_ref[0, rows, :] + window(0)
        lax.fori_loop(1, n_win, more, 0)


def _combine(ye, pos_t, aff, starts, h2, mod, layer, mod_row, tc, win, *, name):
    b, l, d = h2.shape
    e, cap = ye.shape[1], ye.shape[2]
    t = min(ROW_TILE, l)
    mrow = (lambda bb: bb) if mod_row is None else (lambda bb: mod_row)
    assert cap <= 256 and LANE % win == 0
    pos_tf = jnp.pad(pos_t.astype(BF16), ((0, 0), (0, 0), (0, LANE - e)))
    rep = (jnp.arange(LANE)[:, None] == jnp.arange(e * win)[None, :] // win).astype(BF16)
    return pl.pallas_call(
        functools.partial(_combine_kernel, cap=cap, tc=tc, win=win, n_chunks=l // tc),
        grid_spec=pltpu.PrefetchScalarGridSpec(
            num_scalar_prefetch=1,
            grid=(b, l // t),
            in_specs=[
                pl.BlockSpec((1, e, cap, d), lambda bb, i, st: (bb, 0, 0, 0)),
                pl.BlockSpec((1, t, LANE), lambda bb, i, st: (bb, i, 0)),
                pl.BlockSpec((LANE, e * win), lambda bb, i, st: (0, 0)),
                pl.BlockSpec((1, t, LANE), lambda bb, i, st: (bb, i, 0)),
                pl.BlockSpec((1, t, d), lambda bb, i, st: (bb, i, 0)),
                pl.BlockSpec((1, 1, 6, d), lambda bb, i, st: (layer, mrow(bb), 0, 0)),
            ],
            out_specs=pl.BlockSpec((1, t, d), lambda bb, i, st: (bb, i, 0)),
        ),
        out_shape=jax.ShapeDtypeStruct((b, l, d), F32),
        compiler_params=_cp("arbitrary", "arbitrary"),
        name=name,
    )(starts, ye, pos_tf, rep, aff, h2, mod)


_PERM64 = tuple(list(range(0, 16)) + list(range(32, 48)) + list(range(16, 32)) + list(range(48, 64)))


def _rope_tables(seq_len):
    t = jnp.arange(seq_len)
    row = (t // GRID_W).astype(F32)
    col = (t % GRID_W).astype(F32)
    n_freq = C_HEAD_DIM // 4
    inv = ROPE_THETA ** (-jnp.arange(n_freq, dtype=F32) / n_freq)
    ar, ac = row[:, None] * inv[None], col[:, None] * inv[None]
    c64 = jnp.concatenate([jnp.cos(ar), jnp.cos(ac), jnp.cos(ar), jnp.cos(ac)], axis=-1)
    s64 = jnp.concatenate([-jnp.sin(ar), -jnp.sin(ac), jnp.sin(ar), jnp.sin(ac)], axis=-1)
    reps = C_QK // C_HEAD_DIM
    return jnp.tile(c64, (1, reps)), jnp.tile(s64, (1, reps))


def kernel(x, c, ctx, c_ctx, w_ada, b_ada, g_mix, g_ffn, w_in_ab, conv_a, conv_b, conv_b_bias, ln_b_g, ln_b_b, w_out_ab, w_in_cd, g_q, g_k, lam_q1, lam_k1, lam_q2, lam_k2, g_subln, conv_d, hf_w1, hf_b1, hf_freq, hf_w2, hf_b2, hf_w3, hf_bias, w_out_cd, w_router, w_gate, w_up, w_down):
    bsz, seq_lat, d = x.shape
    seq_ctx = ctx.shape[1]
    cap_lat = max(1, (EC_CAPACITY * seq_lat) // N_EXPERTS)
    cap_ctx = max(1, (EC_CAPACITY * seq_ctx) // N_EXPERTS)

    cond = jnp.zeros((MOD_ROWS, d), F32).at[:bsz].set(c).at[CTX_ROW].set(c_ctx)
    mod = _ada(cond, w_ada, b_ada)

    perm = jnp.asarray(_PERM64)

    def permute_qk(w):
        quarter = C_HEAD_DIM // 4
        qk = w[:, :2 * C_QK].reshape(d, 2 * C_QK // C_HEAD_DIM, 2, 2, quarter)
        return jnp.concatenate([jnp.swapaxes(qk, 2, 3).reshape(d, 2 * C_QK), w[:, 2 * C_QK:]], axis=1)

    gi = jnp.arange(C_QK) // C_HEAD_DIM
    gsum = (gi[:, None] == gi[None, :]).astype(BF16)
    cos_t, sin_t = _rope_tables(seq_lat)
    dft_lat, dft_ctx = _dft_mats(seq_lat), _dft_mats(seq_ctx)

    flat = lambda a: a.reshape(1, bsz * seq_ctx, a.shape[-1])

    def proj_in_ctx(h, *args, **kwargs):
        return _proj_in(flat(h), *args, **kwargs).reshape(bsz, seq_ctx, -1)

    h_lat, h_ctx = x, ctx
    for l in range(DEPTH):
        last = l == DEPTH - 1
        odd = l % 2 == 1
        i = l // 2
        gm = g_mix[l][None]
        if not odd:
            w_in = w_in_ab[i].astype(BF16)
            w_out = w_out_ab[i].astype(BF16)
            cargs = (conv_a[i], conv_b[i], conv_b_bias[i], ln_b_g[i], ln_b_b[i])
            p_lat = _proj_in(h_lat, mod, l, None, gm, w_in, name=f"proj_in_lat{l}")
            ys_lat = [_conv_mix(p_lat, *cargs, name=f"conv_mix_lat{l}")]
            if not last:
                p_ctx = proj_in_ctx(h_ctx, mod, l, CTX_ROW, gm, w_in, name=f"proj_in_ctx{l}")
                ys_ctx = [_conv_mix(p_ctx, *cargs, name=f"conv_mix_ctx{l}")]
        else:
            lam_init = 0.8 - 0.6 * math.exp(-0.3 * l)
            w_in = permute_qk(w_in_cd[i]).astype(BF16)
            w_out = w_out_cd[i].astype(BF16)
            gqk = jnp.stack([jnp.tile(g_q[i][perm], C_QK // C_HEAD_DIM) * (C_HEAD_DIM ** -0.5 * math.log2(math.e)),
                             jnp.tile(g_k[i][perm], C_QK // C_HEAD_DIM)])
            lamv = jnp.stack([lam_q1[i], lam_k1[i], lam_q2[i], lam_k2[i]])
            gsub = g_subln[i][None]
            p_lat = _proj_in(h_lat, mod, l, None, gm, w_in, norm_rows=(0, 1), gqk=gqk, gsum=gsum,
                             cos=cos_t, sin=sin_t, name=f"proj_in_lat{l}")
            if last:
                p_ctx = proj_in_ctx(h_ctx, mod, l, CTX_ROW, gm, w_in[:, C_QK:2 * C_QK + C_V], norm_rows=(1,),
                                    gqk=gqk, gsum=gsum, name=f"proj_in_ctx{l}")
                kv_off = 0
            else:
                p_ctx = proj_in_ctx(h_ctx, mod, l, CTX_ROW, gm, w_in, norm_rows=(0, 1), gqk=gqk, gsum=gsum,
                                    name=f"proj_in_ctx{l}")
                kv_off = C_HEADS
            o_lat = _attn(p_lat, p_ctx, kv_off, p_lat, lamv, gsub, lam_init, name=f"attn_lat{l}")
            fargs = (hf_w1[i], hf_b1[i], hf_freq[i], hf_w2[i], hf_b2[i], hf_w3[i])
            filt_lat = _hy_filter(seq_lat, *fargs, name=f"hy_filter_lat{l}")
            ys_lat = [o_lat, _hyena(p_lat, conv_d[i], filt_lat, hf_bias[i], dft_lat, f"lat{l}")]
            if not last:
                o_ctx = _attn(p_ctx, p_ctx, kv_off, None, lamv, gsub, lam_init, name=f"attn_ctx{l}")
                filt_ctx = _hy_filter(seq_ctx, *fargs, name=f"hy_filter_ctx{l}")
                ys_ctx = [o_ctx, _hyena(p_ctx, conv_d[i], filt_ctx, hf_bias[i], dft_ctx, f"ctx{l}")]

        gf = g_ffn[l][None]
        wr = jnp.pad(w_router[l], ((0, 0), (0, LANE - N_EXPERTS)))
        streams = [("lat", h_lat, ys_lat, None, cap_lat)]
        if not last:
            streams.append(("ctx", h_ctx, ys_ctx, CTX_ROW, cap_ctx))
        staged = []
        for tag, h, ys, mrow, cap in streams:
            if mrow is None:
                h2, u2, aff, afft = _proj_out(ys, w_out, h, mod, l, mrow, gf, wr, name=f"proj_out_{tag}{l}")
            else:
                h2, u2, aff, afft = _proj_out([flat(y) for y in ys], w_out, flat(h), mod, l, mrow, gf, wr,
                                              name=f"proj_out_{tag}{l}")
                h2, u2, aff = (a.reshape(bsz, seq_ctx, a.shape[-1]) for a in (h2, u2, aff))
                afft = jnp.transpose(afft.reshape(N_EXPERTS, bsz, seq_ctx), (1, 0, 2))
            tc = min(MOE_CHUNK, h.shape[1])
            win = min(MOE_WINDOW, cap)
            pos, starts = _route(afft, cap, tc, name=f"route_{tag}{l}")
            xe = _gather(u2, pos, starts, cap, tc, win, name=f"gather_{tag}{l}")
            staged.append((tag, h2, aff, pos, starts, xe, mrow, tc, win))
        yes = _ffn([s[5] for s in staged], w_gate, w_up, w_down, l, name=f"ffn{l}")
        outs = []
        for (tag, h2, aff, pos, starts, _, mrow, tc, win), ye in zip(staged, yes):
            pos_t = jnp.transpose(pos, (0, 2, 1))
            outs.append(_combine(ye, pos_t, aff, starts, h2, mod, l, mrow, tc, win, name=f"combine_{tag}{l}"))
        h_lat = outs[0]
        if not last:
            h_ctx = outs[1]
    return h_lat
```

```python
import functools
import math

import jax
import jax.numpy as jnp
from jax import lax
from jax.experimental import pallas as pl
from jax.experimental.pallas import tpu as pltpu

F32 = jnp.float32
BF16 = jnp.bfloat16

D_MODEL = 1024
DEPTH = 4
GRID_W = 64
A_WIDTH = 512
B_WIDTH = 512
SHORT_CONV_W = 3
CONFORMER_CONV_W = 31
AB_IN = 3 * A_WIDTH + 2 * B_WIDTH
C_HEAD_DIM = 64
C_V_DIM = 128
C_HEADS = 4
C_QK = 512
C_V = 512
HY_WIDTH = 512
CD_IN = 2 * C_QK + C_V + 3 * HY_WIDTH
ROPE_THETA = 10000.0
HY_BANDS = 16
HY_EMB = 2 * HY_BANDS + 1
HY_MIN_DECAY = math.log(1e-2) / 1.5
HY_MAX_DECAY = math.log(1e-2) / 0.3
N_EXPERTS = 16
EXPERT_FF = 1024
EC_CAPACITY = 2
NORM_EPS = 1e-6
SUBLN_EPS = 1e-5

LANE = 128
SUBLANE = 8
HALO = 16
CONV_ROWS = 128
ROW_ALIGN = 16
MOE_CHUNK = 256
MOE_WINDOW = 64
ATTN_ROWS = 128
ROW_CHUNK = 256
MOD_ROWS = 16
CTX_ROW = 8
F32_MIN_NORMAL_BITS = 0x00800000

ROW_TILE = 1024
CONV_TILE = 512
FREQ_TILE = 512
PROJ_COLS = 512
ADA_COLS = 1536
FFN_SAMPLES = 4
FFN_COLS = 512
VMEM_LIMIT = 52 * 1024 * 1024
FFN_VMEM_LIMIT = 56 * 1024 * 1024


def _cp(*sem, vmem=VMEM_LIMIT):
    return pltpu.CompilerParams(dimension_semantics=sem, vmem_limit_bytes=vmem)


def _dot(a, b):
    return jnp.dot(a, b, preferred_element_type=F32)


def _dot3(a, b):
    a_hi = a.astype(BF16)
    a_lo = (a - a_hi.astype(F32)).astype(BF16)
    b_hi = b.astype(BF16)
    b_lo = (b - b_hi.astype(F32)).astype(BF16)
    return _dot(a_hi, b_hi) + (_dot(a_hi, b_lo) + _dot(a_lo, b_hi))


def _rms_mod(h, g, shift, scale):
    ms = jnp.mean(h * h, axis=-1, keepdims=True)
    return (h * lax.rsqrt(ms + NORM_EPS) * g) * (1.0 + scale) + shift


def _ada_kernel(c_ref, w_ref, b_ref, o_ref):
    c = c_ref[...]
    s = (c * jax.nn.sigmoid(c)).astype(BF16)
    o_ref[0] = _dot(s, w_ref[0].astype(BF16)) + b_ref[0]


def _ada(cond, w_ada, b_ada):
    depth, d, n = w_ada.shape
    tn = ADA_COLS
    out = pl.pallas_call(
        _ada_kernel,
        grid=(depth, n // tn),
        in_specs=[
            pl.BlockSpec((MOD_ROWS, d), lambda l, j: (0, 0)),
            pl.BlockSpec((1, d, tn), lambda l, j: (l, 0, j)),
            pl.BlockSpec((1, 1, tn), lambda l, j: (l, 0, j)),
        ],
        out_specs=pl.BlockSpec((1, MOD_ROWS, tn), lambda l, j: (l, 0, j)),
        out_shape=jax.ShapeDtypeStruct((depth, MOD_ROWS, n), F32),
        compiler_params=_cp("arbitrary", "arbitrary"),
        name="ada",
    )(cond, w_ada, b_ada.reshape(depth, 1, n))
    return out.reshape(depth, MOD_ROWS, 6, d)


def _swap32(x):
    n = x.shape[-1]
    lane = lax.broadcasted_iota(jnp.int32, x.shape, x.ndim - 1)
    up = pltpu.roll(x, n - 32, x.ndim - 1)
    dn = pltpu.roll(x, 32, x.ndim - 1)
    return jnp.where((lane & 32) == 0, up, dn)


def _proj_in_kernel(*refs, n_out, chunk, norm_rows, rope):
    h_ref, mod_ref, g_ref, w_ref = refs[:4]
    k = 4
    if norm_rows:
        gqk_ref, gsum_ref = refs[k:k + 2]
        k += 2
    if rope:
        cos_ref, sin_ref = refs[k:k + 2]
        k += 2
    o_ref, u_scr = refs[k], refs[k + 1]
    mod = mod_ref[0, 0]
    u = _rms_mod(h_ref[0], g_ref[...], mod[0:1], mod[1:2])
    u_scr[...] = u.astype(BF16)
    for n in range(n_out // chunk):
        p = _dot(u_scr[...], w_ref[:, n * chunk:(n + 1) * chunk])
        if n < len(norm_rows):
            ss = _dot((p * p).astype(BF16), gsum_ref[...])
            r = norm_rows[n]
            p = p * lax.rsqrt(ss * (1.0 / C_HEAD_DIM) + NORM_EPS) * gqk_ref[r:r + 1, :]
            if rope:
                p = p * cos_ref[...] + _swap32(p) * sin_ref[...]
        o_ref[0, :, n * chunk:(n + 1) * chunk] = p.astype(o_ref.dtype)


def _proj_in(h, mod, layer, mod_row, g, w, *, norm_rows=(), gqk=None, gsum=None, cos=None, sin=None, name):
    b, l, d = h.shape
    n_out = w.shape[1]
    t = min(ROW_TILE, l)
    chunk = PROJ_COLS
    rope = cos is not None
    mrow = (lambda bb: bb) if mod_row is None else (lambda bb: mod_row)
    in_specs = [
        pl.BlockSpec((1, t, d), lambda i, bb: (bb, i, 0)),
        pl.BlockSpec((1, 1, 6, d), lambda i, bb: (layer, mrow(bb), 0, 0)),
        pl.BlockSpec((1, d), lambda i, bb: (0, 0)),
        pl.BlockSpec((d, n_out), lambda i, bb: (0, 0)),
    ]
    args = [h, mod, g, w]
    if norm_rows:
        in_specs += [pl.BlockSpec(gqk.shape, lambda i, bb: (0, 0)),
                     pl.BlockSpec(gsum.shape, lambda i, bb: (0, 0))]
        args += [gqk, gsum]
    if rope:
        in_specs += [pl.BlockSpec((t, chunk), lambda i, bb: (i, 0))] * 2
        args += [cos, sin]
    return pl.pallas_call(
        functools.partial(_proj_in_kernel, n_out=n_out, chunk=chunk, norm_rows=tuple(norm_rows), rope=rope),
        grid=(l // t, b),
        in_specs=in_specs,
        out_specs=pl.BlockSpec((1, t, n_out), lambda i, bb: (bb, i, 0)),
        out_shape=jax.ShapeDtypeStruct((b, l, n_out), BF16),
        scratch_shapes=[pltpu.VMEM((t, d), BF16)],
        compiler_params=_cp("arbitrary", "arbitrary"),
        name=name,
    )(*args)


def _halo_specs(t, l, width, col):
    r = t // HALO
    last = l // HALO - 1
    return [
        pl.BlockSpec((1, t, width), lambda bb, i: (bb, i, col)),
        pl.BlockSpec((1, HALO, width), lambda bb, i: (bb, jnp.maximum(i * r - 1, 0), col)),
        pl.BlockSpec((1, HALO, width), lambda bb, i: (bb, jnp.minimum((i + 1) * r, last), col)),
    ]


def _conv_mix_kernel(pc_ref, pp_ref, pn_ref, ca_ref, cb_ref, bb_ref, lg_ref, lb_ref, o_ref,
                     abuf, gbuf, zbuf, sbuf, wbuf, *, t, nt):
    i = pl.program_id(1)
    keep_prev = jnp.where(i == 0, 0.0, 1.0)
    keep_next = jnp.where(i == nt - 1, 0.0, 1.0)

    def prods(p):
        p = p.astype(F32)
        a = p[:, A_WIDTH:2 * A_WIDTH] * p[:, 2 * A_WIDTH:3 * A_WIDTH]
        g = p[:, 3 * A_WIDTH:3 * A_WIDTH + B_WIDTH] * jax.nn.sigmoid(p[:, 3 * A_WIDTH + B_WIDTH:])
        return a, g

    n_slab = B_WIDTH // LANE

    def put_g(r0, n, g):
        for c in range(n_slab):
            gbuf[c, r0:r0 + n, :] = g[:, c * LANE:(c + 1) * LANE]

    a, g = prods(pp_ref[0])
    abuf[0:HALO] = a * keep_prev
    put_g(0, HALO, g * keep_prev)
    a, g = prods(pc_ref[0])
    abuf[HALO:HALO + t] = a
    put_g(HALO, t, g)
    a, g = prods(pn_ref[0])
    abuf[HALO + t:2 * HALO + t] = a * keep_next
    put_g(HALO + t, HALO, g * keep_next)

    ca = ca_ref[...]
    ya = (ca[0:1] * abuf[HALO - 1:HALO - 1 + t] + ca[1:2] * abuf[HALO:HALO + t]
          + ca[2:3] * abuf[HALO + 1:HALO + 1 + t])
    o_ref[0, :, 0:A_WIDTH] = (pc_ref[0, :, 0:A_WIDTH].astype(F32) * ya).astype(o_ref.dtype)

    pad = (CONFORMER_CONV_W - 1) // 2
    rb = CONV_ROWS
    nv = rb // SUBLANE
    tail = gbuf.shape[1] - 2 * HALO - t
    put_g(2 * HALO + t, tail, jnp.zeros((tail, B_WIDTH), F32))

    def shift_rows(i, carry):
        base = pl.multiple_of(i * rb, rb)
        for c in range(n_slab):
            x = gbuf[c, pl.ds(base, rb + SUBLANE), :]
            for r in range(1, SUBLANE):
                sbuf[r - 1, c, pl.ds(base, rb), :] = x[r:r + rb]
        return carry

    lax.fori_loop(0, pl.cdiv(t + 2 * HALO - SUBLANE, rb), shift_rows, 0)
    for k in range(CONFORMER_CONV_W):
        wbuf[k] = jnp.broadcast_to(cb_ref[k:k + 1, :], (SUBLANE, B_WIDTH))

    def conv_rows(i, carry):
        base = pl.multiple_of(i * rb, rb)
        for c in range(n_slab):
            c0 = c * LANE
            acc = jnp.zeros((nv, SUBLANE, LANE), F32) + bb_ref[:, c0:c0 + LANE]
            for k in range(CONFORMER_CONV_W):
                s = HALO - pad + k
                r = s % SUBLANE
                src = gbuf.at[c] if r == 0 else sbuf.at[r - 1, c]
                rows = src[pl.ds(base + (s - r), rb), :].reshape(nv, SUBLANE, LANE)
                acc = acc + wbuf[k, :, c0:c0 + LANE] * rows
            zbuf[pl.ds(base, rb), c0:c0 + LANE] = acc.reshape(rb, LANE)
        return carry

    lax.fori_loop(0, t // rb, conv_rows, 0)
    z = zbuf[...]
    mu = jnp.mean(z, axis=-1, keepdims=True)
    zc = z - mu
    var = jnp.mean(zc * zc, axis=-1, keepdims=True)
    y = zc * lax.rsqrt(var + NORM_EPS) * lg_ref[...] + lb_ref[...]
    o_ref[0, :, A_WIDTH:] = (y * jax.nn.sigmoid(y)).astype(o_ref.dtype)


def _conv_mix(p, conv_a, conv_b, conv_b_bias, ln_g, ln_b, *, name):
    b, l, n = p.shape
    t = min(CONV_TILE, l)
    nt = l // t
    full = lambda shape: pl.BlockSpec(shape, lambda bb, i: (0,) * len(shape))
    return pl.pallas_call(
        functools.partial(_conv_mix_kernel, t=t, nt=nt),
        grid=(b, nt),
        in_specs=_halo_specs(t, l, n, 0) + [
            full(conv_a.shape), full(conv_b.shape), full((1, B_WIDTH)), full((1, B_WIDTH)), full((1, B_WIDTH))],
        out_specs=pl.BlockSpec((1, t, A_WIDTH + B_WIDTH), lambda bb, i: (bb, i, 0)),
        out_shape=jax.ShapeDtypeStruct((b, l, A_WIDTH + B_WIDTH), BF16),
        scratch_shapes=[pltpu.VMEM((t + 2 * HALO, A_WIDTH), F32),
                        pltpu.VMEM((B_WIDTH // LANE, t + 2 * HALO + CONV_ROWS, LANE), F32),
                        pltpu.VMEM((t, B_WIDTH), F32),
                        pltpu.VMEM((SUBLANE - 1, B_WIDTH // LANE, t + 2 * HALO + CONV_ROWS, LANE), F32),
                        pltpu.VMEM((CONFORMER_CONV_W, SUBLANE, B_WIDTH), F32)],
        compiler_params=_cp("arbitrary", "arbitrary"),
        name=name,
    )(p, p, p, conv_a, conv_b, conv_b_bias[None], ln_g[None], ln_b[None])


def _attn_kernel(*refs, lam_init, has_lat):
    if has_lat:
        q_ref, kc_ref, vc_ref, kl_ref, vl_ref, lam_ref, gs_ref, o_ref = refs
    else:
        q_ref, kc_ref, vc_ref, lam_ref, gs_ref, o_ref = refs
    lv = lam_ref[...]
    lam = (jnp.exp(jnp.sum(lv[0:1] * lv[1:2], axis=-1, keepdims=True))
           - jnp.exp(jnp.sum(lv[2:3] * lv[3:4], axis=-1, keepdims=True)) + lam_init)
    q = q_ref[0]
    lane = lax.broadcasted_iota(jnp.int32, q.shape, 1)
    nt = (((1,), (1,)), ((), ()))
    ones_col = lambda v: jnp.concatenate([v, jnp.ones_like(v)], axis=1)
    vc = ones_col(vc_ref[0])
    if has_lat:
        vl = ones_col(vl_ref[0])
    halves = []
    tq = q.shape[0]
    rc = min(ATTN_ROWS, tq)
    for j in range(2):
        qj = jnp.where((lane >= C_HEAD_DIM) == (j == 1), q, jnp.zeros_like(q))
        outs = []
        for r0 in range(0, tq, rc):
            qr = qj[r0:r0 + rc]
            s_c = lax.dot_general(qr, kc_ref[0], nt, preferred_element_type=F32)
            m = jnp.max(s_c, axis=-1, keepdims=True)
            if has_lat:
                s_l = lax.dot_general(qr, kl_ref[0], nt, preferred_element_type=F32)
                m = jnp.maximum(m, jnp.max(s_l, axis=-1, keepdims=True))
            oj = _dot(jnp.exp2(s_c - m).astype(BF16), vc)
            if has_lat:
                oj = oj + _dot(jnp.exp2(s_l - m).astype(BF16), vl)
            outs.append(oj[:, 0:C_V_DIM] * (1.0 / oj[:, C_V_DIM:C_V_DIM + 1]))
        halves.append(jnp.concatenate(outs, axis=0))
    o = halves[0] - lam * halves[1]
    ms = jnp.mean(o * o, axis=-1, keepdims=True)
    o = o * lax.rsqrt(ms + SUBLN_EPS) * gs_ref[...] * (1.0 - lam_init)
    o_ref[0] = o.astype(o_ref.dtype)


def _attn(pq, pkv_ctx, kv_off, p_lat, lamv, g_sub, lam_init, *, name):
    b, lq, _ = pq.shape
    lc = pkv_ctx.shape[1]
    tq = min(ROW_TILE, lq)
    has_lat = p_lat is not None
    in_specs = [
        pl.BlockSpec((1, tq, LANE), lambda bb, h, i: (bb, i, h)),
        pl.BlockSpec((1, lc, LANE), lambda bb, h, i: (bb, 0, kv_off + h)),
        pl.BlockSpec((1, lc, LANE), lambda bb, h, i: (bb, 0, kv_off + C_HEADS + h)),
    ]
    args = [pq, pkv_ctx, pkv_ctx]
    if has_lat:
        ll = p_lat.shape[1]
        in_specs += [pl.BlockSpec((1, ll, LANE), lambda bb, h, i: (bb, 0, C_HEADS + h)),
                     pl.BlockSpec((1, ll, LANE), lambda bb, h, i: (bb, 0, 2 * C_HEADS + h))]
        args += [p_lat, p_lat]
    in_specs += [pl.BlockSpec((4, C_HEAD_DIM), lambda bb, h, i: (0, 0)),
                 pl.BlockSpec((1, C_V_DIM), lambda bb, h, i: (0, 0))]
    args += [lamv, g_sub]
    return pl.pallas_call(
        functools.partial(_attn_kernel, lam_init=lam_init, has_lat=has_lat),
        grid=(b, C_HEADS, lq // tq),
        in_specs=in_specs,
        out_specs=pl.BlockSpec((1, tq, LANE), lambda bb, h, i: (bb, i, h)),
        out_shape=jax.ShapeDtypeStruct((b, lq, C_V), BF16),
        compiler_params=_cp("arbitrary", "arbitrary", "arbitrary"),
        name=name,
    )(*args)


def _hy_pre_kernel(pc_ref, pp_ref, pn_ref, cd_ref, o_ref, buf, *, t, nt):
    i = pl.program_id(1)
    buf[0:HALO] = pp_ref[0].astype(F32) * jnp.where(i == 0, 0.0, 1.0)
    buf[HALO:HALO + t] = pc_ref[0].astype(F32)
    buf[HALO + t:2 * HALO + t] = pn_ref[0].astype(F32) * jnp.where(i == nt - 1, 0.0, 1.0)
    cd = cd_ref[...]
    z = (cd[0:1] * buf[HALO - 1:HALO - 1 + t] + cd[1:2] * buf[HALO:HALO + t]
         + cd[2:3] * buf[HALO + 1:HALO + 1 + t])
    o_ref[0, :, 0:HY_WIDTH] = z[:, 0:HY_WIDTH].astype(o_ref.dtype)
    o_ref[0, :, HY_WIDTH:] = (z[:, HY_WIDTH:2 * HY_WIDTH] * z[:, 2 * HY_WIDTH:]).astype(o_ref.dtype)


def _hy_pre(p, conv_d, *, name):
    b, l, _ = p.shape
    t = min(CONV_TILE, l)
    nt = l // t
    w = 3 * HY_WIDTH
    return pl.pallas_call(
        functools.partial(_hy_pre_kernel, t=t, nt=nt),
        grid=(b, nt),
        in_specs=_halo_specs(t, l, w, 1) + [pl.BlockSpec(conv_d.shape, lambda bb, i: (0, 0))],
        out_specs=pl.BlockSpec((1, t, 2 * HY_WIDTH), lambda bb, i: (bb, i, 0)),
        out_shape=jax.ShapeDtypeStruct((b, l, 2 * HY_WIDTH), BF16),
        scratch_shapes=[pltpu.VMEM((t + 2 * HALO, w), F32)],
        compiler_params=_cp("arbitrary", "arbitrary"),
        name=name,
    )(p, p, p, conv_d)


def _hy_filter_kernel(z_ref, w1_ref, b1_ref, fr_ref, w2_ref, b2_ref, w3_ref, dist_ref, dec_ref, o_ref):
    fr = fr_ref[...]
    hid = jnp.sin(fr * (_dot3(z_ref[...], w1_ref[...]) + b1_ref[...]))
    hid = jnp.sin(fr * (_dot3(hid, w2_ref[...]) + b2_ref[...]))
    h = _dot3(hid, w3_ref[...])
    o_ref[...] = h * jnp.exp(-dist_ref[...] * dec_ref[...])


def _hy_filter(seq_len, w1, b1, freq, w2, b2, w3, *, name):
    t = jnp.arange(seq_len, dtype=F32)
    t_unit = jnp.linspace(0.0, 1.0, seq_len)[:, None]
    bands = jnp.linspace(1e-4, HY_BANDS - 1, HY_BANDS)
    ang = (2.0 * math.pi / seq_len) * t[:, None] * bands[None]
    z = jnp.concatenate([t_unit, jnp.cos(ang), -jnp.sin(ang)], axis=-1)
    centre = seq_len // 2
    dist = (jnp.abs(t - centre) / max(centre, 1))[:, None]
    decay = jnp.abs(jnp.linspace(HY_MIN_DECAY, HY_MAX_DECAY, HY_WIDTH))[None]
    pad_c = lambda a, n: jnp.pad(a, ((0, 0), (0, n - a.shape[1])))
    pad_r = lambda a, n: jnp.pad(a, ((0, n - a.shape[0]), (0, 0)))
    args = [pad_c(z, LANE), pad_c(pad_r(w1, LANE), LANE), pad_c(b1[None], LANE), pad_c(freq[None], LANE),
            pad_c(pad_r(w2, LANE), LANE), pad_c(b2[None], LANE), pad_r(w3, LANE), dist, decay]
    return pl.pallas_call(
        _hy_filter_kernel,
        out_shape=jax.ShapeDtypeStruct((seq_len, HY_WIDTH), F32),
        compiler_params=pltpu.CompilerParams(vmem_limit_bytes=VMEM_LIMIT),
        name=name,
    )(*args)


def _dft_mats(l):
    m = 3 * l // 2
    hh = m // 2
    f = jnp.arange(hh, dtype=jnp.int32)
    s = jnp.arange(l, dtype=jnp.int32)
    n = s + l // 2

    def cos_sin(rows, n_cols):
        w = 64
        phase = lambda k: (2.0 * math.pi / m) * ((rows[:, None] * k[None, :]) % m).astype(F32)
        a = phase(jnp.arange(0, n_cols, w, dtype=jnp.int32))[:, :, None]
        b = phase(jnp.arange(w, dtype=jnp.int32))[:, None, :]
        ca, sa, cb, sb = jnp.cos(a), jnp.sin(a), jnp.cos(b), jnp.sin(b)
        shape = (rows.shape[0], n_cols)
        return (ca * cb - sa * sb).reshape(shape), (sa * cb + ca * sb).reshape(shape)

    c, sn = cos_sin(f, l)
    nyq = jnp.where(s % 2 == 0, 1.0, -1.0).astype(F32)
    fw = jnp.stack([c, jnp.where(f[:, None] == 0, nyq[None, :], -sn)]).astype(BF16)
    c, sn = cos_sin(n, hh)
    nyq_n = jnp.where(n % 2 == 0, 1.0, -1.0).astype(F32)
    iv_r = jnp.where(f[None, :] == 0, 1.0, 2.0 * c)
    iv_i = jnp.where(f[None, :] == 0, nyq_n[:, None], -2.0 * sn)
    iv = jnp.concatenate([iv_r, iv_i], axis=1).astype(BF16)
    return fw, iv


def _dft_fwd_kernel(*refs, tf):
    fw_ref, v_ref = refs[:2]
    o_ref = refs[-1]
    v = v_ref[0].astype(BF16)
    ur = _dot(fw_ref[0], v)
    ui = _dot(fw_ref[1], v)
    if len(refs) == 3:
        o_ref[0, 0] = ur.astype(o_ref.dtype)
        o_ref[0, 1] = ui.astype(o_ref.dtype)
        return
    h_ref = refs[2]
    hr = h_ref[0]
    hi = h_ref[1]
    row = lax.broadcasted_iota(jnp.int32, ur.shape, 0) + pl.program_id(0) * tf
    packed = row == 0
    uihi = ui * hi
    o_ref[0, 0] = (ur * hr - jnp.where(packed, 0.0, uihi)).astype(o_ref.dtype)
    o_ref[0, 1] = jnp.where(packed, uihi, ur * hi + ui * hr).astype(o_ref.dtype)


def _dft_fwd(fw, v, col, hspec, out_dtype, *, name):
    b, l, _ = v.shape
    hh = fw.shape[1]
    tf = min(FREQ_TILE, hh)
    in_specs = [
        pl.BlockSpec((2, tf, l), lambda i, bb: (0, i, 0)),
        pl.BlockSpec((1, l, HY_WIDTH), lambda i, bb: (bb, 0, col)),
    ]
    args = [fw, v]
    if hspec is not None:
        in_specs.append(pl.BlockSpec((2, tf, HY_WIDTH), lambda i, bb: (0, i, 0)))
        args.append(hspec)
    return pl.pallas_call(
        functools.partial(_dft_fwd_kernel, tf=tf),
        grid=(hh // tf, b),
        in_specs=in_specs,
        out_specs=pl.BlockSpec((1, 2, tf, HY_WIDTH), lambda i, bb: (bb, 0, i, 0)),
        out_shape=jax.ShapeDtypeStruct((b, 2, hh, HY_WIDTH), out_dtype),
        compiler_params=_cp("arbitrary", "arbitrary"),
        name=name,
    )(*args)


def _dft_inv_kernel(iv_ref, y_ref, gv_ref, bias_ref, o_ref, *, inv_m):
    m = iv_ref.shape[1]
    yhat = y_ref[0].reshape(m, HY_WIDTH)
    conv = _dot(iv_ref[...], yhat) * inv_m
    gv = gv_ref[0].astype(F32)
    o_ref[0] = (gv[:, 0:HY_WIDTH] * (conv + gv[:, HY_WIDTH:] * bias_ref[...])).astype(o_ref.dtype)


def _dft_inv(iv, yhat, gv, bias, *, name):
    b, l, _ = gv.shape
    m = iv.shape[1]
    tn = min(ROW_TILE, l)
    return pl.pallas_call(
        functools.partial(_dft_inv_kernel, inv_m=1.0 / m),
        grid=(l // tn, b),
        in_specs=[
            pl.BlockSpec((tn, m), lambda i, bb: (i, 0)),
            pl.BlockSpec((1, 2, m // 2, HY_WIDTH), lambda i, bb: (bb, 0, 0, 0)),
            pl.BlockSpec((1, tn, 2 * HY_WIDTH), lambda i, bb: (bb, i, 0)),
            pl.BlockSpec((1, HY_WIDTH), lambda i, bb: (0, 0)),
        ],
        out_specs=pl.BlockSpec((1, tn, HY_WIDTH), lambda i, bb: (bb, i, 0)),
        out_shape=jax.ShapeDtypeStruct((b, l, HY_WIDTH), BF16),
        compiler_params=_cp("arbitrary", "arbitrary"),
        name=name,
    )(iv, yhat, gv, bias)


def _hyena(p, conv_d, filt, bias, dft, tag):
    fw, iv = dft
    fhat = _dft_fwd(fw, filt[None], 0, None, F32, name="hy_fhat_" + tag)[0]
    gv = _hy_pre(p, conv_d, name="hy_pre_" + tag)
    yhat = _dft_fwd(fw, gv, 1, fhat, BF16, name="hy_fwd_" + tag)
    return _dft_inv(iv, yhat, gv, bias[None], name="hy_inv_" + tag)


def _proj_out_kernel(*refs, n_y):
    y_refs = refs[:n_y]
    w_ref, h_ref, mod_ref, g_ref, wr_ref, h2_ref, u2_ref, aff_ref, afft_ref = refs[n_y:]
    mod = mod_ref[0, 0]
    wr = wr_ref[...]
    wr_hi = wr.astype(BF16).astype(F32)
    wr_hl = (wr_hi + pltpu.roll(wr - wr_hi, N_EXPERTS, 1)).astype(BF16)
    t = h_ref.shape[1]
    rc = min(ROW_CHUNK, t)
    for r0 in range(0, t, rc):
        acc = None
        k0 = 0
        for y_ref in y_refs:
            kk = y_ref.shape[2]
            part = _dot(y_ref[0, r0:r0 + rc, :], w_ref[k0:k0 + kk, :])
            acc = part if acc is None else acc + part
            k0 += kk
        h2 = h_ref[0, r0:r0 + rc, :] + mod[2:3] * acc
        h2_ref[0, r0:r0 + rc, :] = h2
        u2 = _rms_mod(h2, g_ref[...], mod[3:4], mod[4:5])
        u2_ref[0, r0:r0 + rc, :] = u2.astype(u2_ref.dtype)
        u_hi = u2.astype(BF16)
        u_lo = (u2 - u_hi.astype(F32)).astype(BF16)
        parts = _dot(u_hi, wr_hl) + _dot(u_lo, wr_hl)
        logits = parts + pltpu.roll(parts, LANE - N_EXPERTS, 1)
        lane = lax.broadcasted_iota(jnp.int32, logits.shape, 1)
        logits = jnp.where(lane < N_EXPERTS, logits, -jnp.inf)
        e = jnp.exp(logits - jnp.max(logits, axis=-1, keepdims=True))
        aff = e / jnp.sum(e, axis=-1, keepdims=True)
        aff_ref[0, r0:r0 + rc, :] = aff
        afft_ref[0, :, r0:r0 + rc] = aff.T[0:N_EXPERTS, :]


def _proj_out(ys, w, h, mod, layer, mod_row, g, wr, *, name):
    b, l, d = h.shape
    t = min(ROW_TILE, l)
    mrow = (lambda bb: bb) if mod_row is None else (lambda bb: mod_row)
    in_specs = [pl.BlockSpec((1, t, y.shape[2]), lambda bb, i: (bb, i, 0)) for y in ys] + [
        pl.BlockSpec(w.shape, lambda bb, i: (0, 0)),
        pl.BlockSpec((1, t, d), lambda bb, i: (bb, i, 0)),
        pl.BlockSpec((1, 1, 6, d), lambda bb, i: (layer, mrow(bb), 0, 0)),
        pl.BlockSpec((1, d), lambda bb, i: (0, 0)),
        pl.BlockSpec((d, LANE), lambda bb, i: (0, 0)),
    ]
    return pl.pallas_call(
        functools.partial(_proj_out_kernel, n_y=len(ys)),
        grid=(b, l // t),
        in_specs=in_specs,
        out_specs=[
            pl.BlockSpec((1, t, d), lambda bb, i: (bb, i, 0)),
            pl.BlockSpec((1, t, d), lambda bb, i: (bb, i, 0)),
            pl.BlockSpec((1, t, LANE), lambda bb, i: (bb, i, 0)),
            pl.BlockSpec((1, N_EXPERTS, t), lambda bb, i: (bb, 0, i)),
        ],
        out_shape=[
            jax.ShapeDtypeStruct((b, l, d), F32),
            jax.ShapeDtypeStruct((b, l, d), BF16),
            jax.ShapeDtypeStruct((b, l, LANE), F32),
            jax.ShapeDtypeStruct((b, N_EXPERTS, l), F32),
        ],
        compiler_params=_cp("arbitrary", "arbitrary"),
        name=name,
    )(*ys, w, h, mod, g, wr)


def _excl_cumsum(mask):
    r, l = mask.shape
    x = jnp.where(mask, 1.0, 0.0)
    i0 = lax.broadcasted_iota(jnp.int32, (LANE, LANE), 0)
    i1 = lax.broadcasted_iota(jnp.int32, (LANE, LANE), 1)
    tri = jnp.where(i0 < i1, 1.0, 0.0).astype(BF16)
    carry = jnp.zeros((r, 1), F32)
    outs = []
    for c in range(l // LANE):
        xc = x[:, c * LANE:(c + 1) * LANE]
        outs.append(_dot(xc.astype(BF16), tri) + carry)
        carry = carry + jnp.sum(xc, axis=-1, keepdims=True)
    return jnp.concatenate(outs, axis=1)


def _route_kernel(a_ref, pos_ref, post_ref, starts_ref, *, cap, tc):
    aff = a_ref[...]
    e, l = aff.shape
    capf = jnp.float32(cap)
    count = lambda mask: jnp.sum(jnp.where(mask, 1.0, 0.0), axis=-1, keepdims=True)

    def body(i, thr):
        cand = thr | jnp.left_shift(jnp.int32(1), 30 - i)
        return jnp.where(count(aff >= pltpu.bitcast(cand, F32)) >= capf, cand, thr)

    thr = lax.fori_loop(0, 31, body, jnp.zeros((e, 1), jnp.int32))
    t_lo = pltpu.bitcast(thr, F32)
    t_hi = pltpu.bitcast(jnp.maximum(thr + 1, F32_MIN_NORMAL_BITS), F32)
    above = aff >= t_hi
    sel0 = jnp.where(above, 1.0, 0.0)
    mid0 = jnp.where((aff >= t_lo) & jnp.logical_not(above), 1.0, 0.0)
    lane = lax.broadcasted_iota(jnp.int32, aff.shape, 1).astype(F32)

    def fill(carry):
        sel, mid, need = carry
        v = jnp.where(mid > 0.0, aff, -1.0)
        best = jnp.max(v, axis=-1, keepdims=True)
        first = jnp.min(jnp.where(v == best, lane, jnp.float32(l)), axis=-1, keepdims=True)
        pick = (lane == first) & (need > 0.0)
        return (jnp.where(pick, 1.0, sel), jnp.where(pick, 0.0, mid), need - jnp.where(need > 0.0, 1.0, 0.0))

    sel, _, _ = lax.while_loop(lambda carry: jnp.max(carry[2]) > 0.0, fill, (sel0, mid0, capf - count(above)))
    chosen = sel > 0.0
    pos = _excl_cumsum(chosen)
    pos_ref[...] = jnp.where(chosen, pos.astype(jnp.int32), -1)
    post_ref[...] = jnp.where(chosen, pos, -1.0).T.astype(BF16)
    chunk = lax.broadcasted_iota(jnp.int32, (e, LANE), 1)
    starts = jnp.zeros((e, LANE), F32)
    for c in range(l // tc):
        starts = starts + jnp.where(chunk > c, count(chosen[:, c * tc:(c + 1) * tc]), 0.0)
    starts_ref[...] = starts.astype(jnp.int32)


def _route(afft, cap, tc, *, name):
    b, e, l = afft.shape
    assert b * e == LANE and cap <= 256
    pos, pos_t, starts = pl.pallas_call(
        functools.partial(_route_kernel, cap=cap, tc=tc),
        out_shape=[jax.ShapeDtypeStruct((b * e, l), jnp.int32),
                   jax.ShapeDtypeStruct((l, b * e), BF16),
                   jax.ShapeDtypeStruct((b * e, LANE), jnp.int32)],
        compiler_params=pltpu.CompilerParams(vmem_limit_bytes=VMEM_LIMIT),
        name=name,
    )(afft.reshape(b * e, l))
    return pos.reshape(b, e, l), pos_t, starts[:, :l // tc + 1].reshape(-1)


def _windows(st_ref, b, c, n_e, n_chunks, win):
    starts, n_win = [], None
    for e in range(n_e):
        k = (b * n_e + e) * (n_chunks + 1) + c
        start = st_ref[k] & -ROW_ALIGN
        need = (st_ref[k + 1] - start + (win - 1)) // win
        starts.append(start)
        n_win = need if n_win is None else jnp.maximum(n_win, need)
    return starts, n_win


def _window(starts, e, w, cap, win):
    first = starts[e] + w * win
    return first, pl.multiple_of(jnp.minimum(first, cap - win), ROW_ALIGN)


def _gather_kernel(st_ref, u_ref, pos_ref, o_ref, *, cap, tc, win):
    b = pl.program_id(0)
    n_e = pos_ref.shape[1]
    n_chunks = u_ref.shape[1] // tc
    o_ref[...] = jnp.zeros(o_ref.shape, o_ref.dtype)
    slot = lax.broadcasted_iota(jnp.int32, (win, tc), 0)
    for c in range(n_chunks):
        x = u_ref[0, c * tc:(c + 1) * tc, :]
        pos = pos_ref[0, :, c * tc:(c + 1) * tc]
        starts, n_win = _windows(st_ref, b, c, n_e, n_chunks, win)

        def window(w, carry):
            blocks, at = [], []
            for e in range(n_e):
                first, row0 = _window(starts, e, w, cap, win)
                idx = slot + row0
                blocks.append(jnp.where((pos[e:e + 1] == idx) & (idx >= first), 1.0, 0.0).astype(BF16))
                at.append(row0)
            y = _dot(jnp.concatenate(blocks, axis=0), x).astype(o_ref.dtype)
            for e in range(n_e):
                rows = pl.ds(at[e], win)
                o_ref[0, e, rows, :] = o_ref[0, e, rows, :] + y[e * win:(e + 1) * win]
            return carry

        lax.fori_loop(0, n_win, window, 0)


def _gather(u2, pos, starts, cap, tc, win, *, name):
    b, l, d = u2.shape
    e = pos.shape[1]
    return pl.pallas_call(
        functools.partial(_gather_kernel, cap=cap, tc=tc, win=win),
        grid_spec=pltpu.PrefetchScalarGridSpec(
            num_scalar_prefetch=1,
            grid=(b,),
            in_specs=[
                pl.BlockSpec((1, l, d), lambda bb, st: (bb, 0, 0)),
                pl.BlockSpec((1, e, l), lambda bb, st: (bb, 0, 0)),
            ],
            out_specs=pl.BlockSpec((1, e, cap, d), lambda bb, st: (bb, 0, 0, 0)),
        ),
        out_shape=jax.ShapeDtypeStruct((b, e, cap, d), BF16),
        compiler_params=_cp("arbitrary"),
        name=name,
    )(starts, u2, pos)


def _ffn_kernel(*refs, n_x, fc):
    x_refs = refs[:n_x]
    wg_ref, wu_ref, wd_ref = refs[n_x:n_x + 3]
    y_refs = refs[n_x + 3:2 * n_x + 3]
    wg_s, wu_s, wd_s = refs[2 * n_x + 3:]

    @pl.when(pl.program_id(1) == 0)
    def _():
        wg_s[...] = wg_ref[0, 0].astype(BF16)
        wu_s[...] = wu_ref[0, 0].astype(BF16)
        wd_s[...] = wd_ref[0, 0].astype(BF16)

    d = x_refs[0].shape[3]
    rows = [r.shape[0] * r.shape[2] for r in x_refs]
    x = jnp.concatenate([r[:, 0].reshape(n, d) for r, n in zip(x_refs, rows)], axis=0)
    y = None
    for f0 in range(0, wg_s.shape[1], fc):
        a = _dot(x, wg_s[:, f0:f0 + fc])
        up = _dot(x, wu_s[:, f0:f0 + fc])
        hm = (a * jax.nn.sigmoid(a) * up).astype(BF16)
        part = _dot(hm, wd_s[f0:f0 + fc, :])
        y = part if y is None else y + part
    r0 = 0
    for y_ref, n in zip(y_refs, rows):
        y_ref[:, 0] = y[r0:r0 + n].reshape(y_ref.shape[0], y_ref.shape[2], d).astype(y_ref.dtype)
        r0 += n


def _ffn(xs, w_gate, w_up, w_down, layer, *, name):
    b, e, _, d = xs[0].shape
    f = w_gate.shape[3]
    bb = FFN_SAMPLES
    xspec = lambda x: pl.BlockSpec((bb, 1, x.shape[2], d), lambda ee, j: (j, ee, 0, 0))
    return pl.pallas_call(
        functools.partial(_ffn_kernel, n_x=len(xs), fc=FFN_COLS),
        grid=(e, b // bb),
        in_specs=[xspec(x) for x in xs] + [
            pl.BlockSpec((1, 1, d, f), lambda ee, j: (layer, ee, 0, 0)),
            pl.BlockSpec((1, 1, d, f), lambda ee, j: (layer, ee, 0, 0)),
            pl.BlockSpec((1, 1, f, d), lambda ee, j: (layer, ee, 0, 0)),
        ],
        out_specs=[xspec(x) for x in xs],
        out_shape=[jax.ShapeDtypeStruct(x.shape, BF16) for x in xs],
        scratch_shapes=[pltpu.VMEM((d, f), BF16), pltpu.VMEM((d, f), BF16), pltpu.VMEM((f, d), BF16)],
        compiler_params=_cp("arbitrary", "arbitrary", vmem=FFN_VMEM_LIMIT),
        name=name,
    )(*xs, w_gate, w_up, w_down)


def _combine_kernel(st_ref, ye_ref, ptf_ref, reps_ref, repg_ref, aff_ref, h_ref, mod_ref, o_ref,
                    *, cap, tc, win, n_chunks):
    b = pl.program_id(0)
    n_e = ye_ref.shape[1]
    t = h_ref.shape[1]
    gate = mod_ref[0, 0][5:6]
    lane = lax.broadcasted_iota(jnp.int32, (1, n_e * win), 1)
    for cc in range(t // tc):
        c = pl.program_id(1) * (t // tc) + cc
        rows = slice(cc * tc, (cc + 1) * tc)
        starts, n_win = _windows(st_ref, b, c, n_e, n_chunks, win)
        slot_of = _dot(ptf_ref[rows, :], reps_ref[0])
        gate_of = _dot(aff_ref[0, rows, :].astype(BF16), repg_ref[...])

        def window(w):
            ys = []
            want = lane % win
            valid = lane < 0
            for e in range(n_e):
                first, row0 = _window(starts, e, w, cap, win)
                ys.append(ye_ref[0, e, pl.ds(row0, win), :])
                mine = lane // win == e
                want = jnp.where(mine, want + row0, want)
                valid = valid | (mine & (lane % win + row0 >= first))
            gates = jnp.where((slot_of == want.astype(F32)) & valid, gate_of, 0.0).astype(BF16)
            return gate * _dot(gates, jnp.concatenate(ys, axis=0))

        def more(w, carry):
            o_ref[0, rows, :] = o_ref[0, rows, :] + window(w)
            return carry

        o_ref[0, rows, :] = h_ref[0, rows, :] + window(0)
        lax.fori_loop(1, n_win, more, 0)


def _combine(ye, pos_t, aff, starts, h2, mod, layer, mod_row, tc, win, *, name):
    b, l, d = h2.shape
    e, cap = ye.shape[1], ye.shape[2]
    t = min(ROW_TILE, l)
    mrow = (lambda bb: bb) if mod_row is None else (lambda bb: mod_row)
    assert LANE % win == 0
    owner = jnp.arange(e * win)[None, :] // win
    rep_gate = (jnp.arange(LANE)[:, None] == owner).astype(BF16)
    rep_slot = (jnp.arange(LANE)[None, :, None] == jnp.arange(b)[:, None, None] * e + owner[None]).astype(BF16)
    return pl.pallas_call(
        functools.partial(_combine_kernel, cap=cap, tc=tc, win=win, n_chunks=l // tc),
        grid_spec=pltpu.PrefetchScalarGridSpec(
            num_scalar_prefetch=1,
            grid=(b, l // t),
            in_specs=[
                pl.BlockSpec((1, e, cap, d), lambda bb, i, st: (bb, 0, 0, 0)),
                pl.BlockSpec((t, LANE), lambda bb, i, st: (i, 0)),
                pl.BlockSpec((1, LANE, e * win), lambda bb, i, st: (bb, 0, 0)),
                pl.BlockSpec((LANE, e * win), lambda bb, i, st: (0, 0)),
                pl.BlockSpec((1, t, LANE), lambda bb, i, st: (bb, i, 0)),
                pl.BlockSpec((1, t, d), lambda bb, i, st: (bb, i, 0)),
                pl.BlockSpec((1, 1, 6, d), lambda bb, i, st: (layer, mrow(bb), 0, 0)),
            ],
            out_specs=pl.BlockSpec((1, t, d), lambda bb, i, st: (bb, i, 0)),
        ),
        out_shape=jax.ShapeDtypeStruct((b, l, d), F32),
        compiler_params=_cp("arbitrary", "arbitrary"),
        name=name,
    )(starts, ye, pos_t, rep_slot, rep_gate, aff, h2, mod)


_PERM64 = tuple(list(range(0, 16)) + list(range(32, 48)) + list(range(16, 32)) + list(range(48, 64)))


def _rope_tables(seq_len):
    t = jnp.arange(seq_len)
    row = (t // GRID_W).astype(F32)
    col = (t % GRID_W).astype(F32)
    n_freq = C_HEAD_DIM // 4
    inv = ROPE_THETA ** (-jnp.arange(n_freq, dtype=F32) / n_freq)
    ar, ac = row[:, None] * inv[None], col[:, None] * inv[None]
    c64 = jnp.concatenate([jnp.cos(ar), jnp.cos(ac), jnp.cos(ar), jnp.cos(ac)], axis=-1)
    s64 = jnp.concatenate([-jnp.sin(ar), -jnp.sin(ac), jnp.sin(ar), jnp.sin(ac)], axis=-1)
    reps = C_QK // C_HEAD_DIM
    return jnp.tile(c64, (1, reps)), jnp.tile(s64, (1, reps))


def kernel(x, c, ctx, c_ctx, w_ada, b_ada, g_mix, g_ffn, w_in_ab, conv_a, conv_b, conv_b_bias, ln_b_g, ln_b_b, w_out_ab, w_in_cd, g_q, g_k, lam_q1, lam_k1, lam_q2, lam_k2, g_subln, conv_d, hf_w1, hf_b1, hf_freq, hf_w2, hf_b2, hf_w3, hf_bias, w_out_cd, w_router, w_gate, w_up, w_down):
    bsz, seq_lat, d = x.shape
    seq_ctx = ctx.shape[1]
    cap_lat = max(1, (EC_CAPACITY * seq_lat) // N_EXPERTS)
    cap_ctx = max(1, (EC_CAPACITY * seq_ctx) // N_EXPERTS)

    cond = jnp.zeros((MOD_ROWS, d), F32).at[:bsz].set(c).at[CTX_ROW].set(c_ctx)
    mod = _ada(cond, w_ada, b_ada)

    perm = jnp.asarray(_PERM64)

    def permute_qk(w):
        quarter = C_HEAD_DIM // 4
        qk = w[:, :2 * C_QK].reshape(d, 2 * C_QK // C_HEAD_DIM, 2, 2, quarter)
        return jnp.concatenate([jnp.swapaxes(qk, 2, 3).reshape(d, 2 * C_QK), w[:, 2 * C_QK:]], axis=1)

    gi = jnp.arange(C_QK) // C_HEAD_DIM
    gsum = (gi[:, None] == gi[None, :]).astype(BF16)
    cos_t, sin_t = _rope_tables(seq_lat)
    dft_lat, dft_ctx = _dft_mats(seq_lat), _dft_mats(seq_ctx)

    flat = lambda a: a.reshape(1, bsz * seq_ctx, a.shape[-1])

    def proj_in_ctx(h, *args, **kwargs):
        return _proj_in(flat(h), *args, **kwargs).reshape(bsz, seq_ctx, -1)

    h_lat, h_ctx = x, ctx
    for l in range(DEPTH):
        last = l == DEPTH - 1
        odd = l % 2 == 1
        i = l // 2
        gm = g_mix[l][None]
        if not odd:
            w_in = w_in_ab[i].astype(BF16)
            w_out = w_out_ab[i].astype(BF16)
            cargs = (conv_a[i], conv_b[i], conv_b_bias[i], ln_b_g[i], ln_b_b[i])
            p_lat = _proj_in(h_lat, mod, l, None, gm, w_in, name=f"proj_in_lat{l}")
            ys_lat = [_conv_mix(p_lat, *cargs, name=f"conv_mix_lat{l}")]
            if not last:
                p_ctx = proj_in_ctx(h_ctx, mod, l, CTX_ROW, gm, w_in, name=f"proj_in_ctx{l}")
                ys_ctx = [_conv_mix(p_ctx, *cargs, name=f"conv_mix_ctx{l}")]
        else:
            lam_init = 0.8 - 0.6 * math.exp(-0.3 * l)
            w_in = permute_qk(w_in_cd[i]).astype(BF16)
            w_out = w_out_cd[i].astype(BF16)
            gqk = jnp.stack([jnp.tile(g_q[i][perm], C_QK // C_HEAD_DIM) * (C_HEAD_DIM ** -0.5 * math.log2(math.e)),
                             jnp.tile(g_k[i][perm], C_QK // C_HEAD_DIM)])
            lamv = jnp.stack([lam_q1[i], lam_k1[i], lam_q2[i], lam_k2[i]])
            gsub = g_subln[i][None]
            p_lat = _proj_in(h_lat, mod, l, None, gm, w_in, norm_rows=(0, 1), gqk=gqk, gsum=gsum,
                             cos=cos_t, sin=sin_t, name=f"proj_in_lat{l}")
            if last:
                p_ctx = proj_in_ctx(h_ctx, mod, l, CTX_ROW, gm, w_in[:, C_QK:2 * C_QK + C_V], norm_rows=(1,),
                                    gqk=gqk, gsum=gsum, name=f"proj_in_ctx{l}")
                kv_off = 0
            else:
                p_ctx = proj_in_ctx(h_ctx, mod, l, CTX_ROW, gm, w_in, norm_rows=(0, 1), gqk=gqk, gsum=gsum,
                                    name=f"proj_in_ctx{l}")
                kv_off = C_HEADS
            o_lat = _attn(p_lat, p_ctx, kv_off, p_lat, lamv, gsub, lam_init, name=f"attn_lat{l}")
            fargs = (hf_w1[i], hf_b1[i], hf_freq[i], hf_w2[i], hf_b2[i], hf_w3[i])
            filt_lat = _hy_filter(seq_lat, *fargs, name=f"hy_filter_lat{l}")
            ys_lat = [o_lat, _hyena(p_lat, conv_d[i], filt_lat, hf_bias[i], dft_lat, f"lat{l}")]
            if not last:
                o_ctx = _attn(p_ctx, p_ctx, kv_off, None, lamv, gsub, lam_init, name=f"attn_ctx{l}")
                filt_ctx = _hy_filter(seq_ctx, *fargs, name=f"hy_filter_ctx{l}")
                ys_ctx = [o_ctx, _hyena(p_ctx, conv_d[i], filt_ctx, hf_bias[i], dft_ctx, f"ctx{l}")]

        gf = g_ffn[l][None]
        wr = jnp.pad(w_router[l], ((0, 0), (0, LANE - N_EXPERTS)))
        streams = [("lat", h_lat, ys_lat, None, cap_lat)]
        if not last:
            streams.append(("ctx", h_ctx, ys_ctx, CTX_ROW, cap_ctx))
        staged = []
        for tag, h, ys, mrow, cap in streams:
            if mrow is None:
                h2, u2, aff, afft = _proj_out(ys, w_out, h, mod, l, mrow, gf, wr, name=f"proj_out_{tag}{l}")
            else:
                h2, u2, aff, afft = _proj_out([flat(y) for y in ys], w_out, flat(h), mod, l, mrow, gf, wr,
                                              name=f"proj_out_{tag}{l}")
                h2, u2, aff = (a.reshape(bsz, seq_ctx, a.shape[-1]) for a in (h2, u2, aff))
                afft = jnp.transpose(afft.reshape(N_EXPERTS, bsz, seq_ctx), (1, 0, 2))
            tc = min(MOE_CHUNK, h.shape[1])
            win = min(MOE_WINDOW, cap)
            pos, pos_t, starts = _route(afft, cap, tc, name=f"route_{tag}{l}")
            xe = _gather(u2, pos, starts, cap, tc, win, name=f"gather_{tag}{l}")
            staged.append((tag, h2, aff, pos_t, starts, xe, mrow, tc, win))
        yes = _ffn([s[5] for s in staged], w_gate, w_up, w_down, l, name=f"ffn{l}")
        outs = []
        for (tag, h2, aff, pos_t, starts, _, mrow, tc, win), ye in zip(staged, yes):
            outs.append(_combine(ye, pos_t, aff, starts, h2, mod, l, mrow, tc, win, name=f"combine_{tag}{l}"))
        h_lat = outs[0]
        if not last:
            h_ctx = outs[1]
    return h_lat
```
